```python
import jax
import jax.numpy as jnp
from jax import lax
import numpy as np

D_MODEL = 1024
BATCH = 32
SEQ = 2048
DEPTH = 4
DEC_BATCH = 4
DEC_SEQ = 8192
PAST_LEN = 128

N_MIXERS = 2
N_RWKV_LAYERS = (DEPTH + 1) // 2
N_NA_LAYERS = DEPTH // 2
HEAD_DIM = 64
N_HEADS = D_MODEL // HEAD_DIM
D_FF = 4 * D_MODEL
D_DECAY_LORA = 64
D_ICLR_LORA = 64
D_VRES_LORA = 32
D_GATE_LORA = 160
GRID_W = 64
WIN_ROWS_MAX = 8
WIN_COLS = 16
STRIP_COLS = 2 * WIN_COLS
N_COL_BLOCKS = GRID_W // WIN_COLS
RMS_EPS = 1e-6
GN_EPS = 64e-5
L2_EPS = 1e-24

kernel_name = 'hybrid_rwkv7_natten_encoder'


def _rms_norm(x, g):
    xf = x.astype(jnp.float32)
    y = xf * lax.rsqrt(jnp.mean(jnp.square(xf), axis=-1, keepdims=True) + RMS_EPS)
    return y.astype(x.dtype) * g


def _heads(z):
    return z.reshape(z.shape[:-1] + (N_HEADS, HEAD_DIM))


def _sq_relu_mlp(h, w_up, w_down):
    return jnp.square(jax.nn.relu(h @ w_up)) @ w_down


def _wkv_scan(r, w, k, v, a, b, reverse):
    B, T, H, N = r.shape

    def step(S, inp):
        r_t, w_t, k_t, v_t, a_t, b_t = inp
        sa = jnp.einsum('bhvk,bhk->bhv', S, a_t)
        S = S * w_t[:, :, None, :] + sa[..., None] * b_t[:, :, None, :] + v_t[..., None] * k_t[:, :, None, :]
        return S, jnp.einsum('bhvk,bhk->bhv', S, r_t)

    seq = tuple(jnp.moveaxis(z, 1, 0) for z in (r, w, k, v, a, b))
    s0 = jnp.zeros((B, H, N, N), jnp.float32)
    _, y = lax.scan(step, s0, seq, reverse=reverse)
    return jnp.moveaxis(y, 0, 1)


def _rwkv7_mixer(h, v_first, vres, mix, w_rkv, w0, w1, w2, a0, a1, a2, g1, g2, k_k, k_a, r_k, lnx_g, lnx_b, w_o):
    B, T, C = h.shape
    f32 = jnp.float32
    zero = jnp.zeros_like(h[:, :1])
    xx = 0.5 * (jnp.concatenate([zero, h[:, :-1]], axis=1) + jnp.concatenate([h[:, 1:], zero], axis=1)) - h
    r, k, v = jnp.einsum('sbtc,scd->sbtd', h[None] + xx[None] * mix[:3, None, None, :], w_rkv)
    xw = h + xx * mix[3]
    xa = h + xx * mix[4]
    xg = h + xx * mix[5]
    if vres is None:
        v_first = v
    else:
        v0, v1, v2 = vres
        xv = h + xx * mix[2]
        v = v + (v_first - v) * jax.nn.sigmoid(v0 + (xv @ v1) @ v2)
    g = jax.nn.sigmoid(xg @ g1) @ g2
    rf, kf, vf = _heads(r.astype(f32)), _heads(k.astype(f32)), _heads(v.astype(f32))
    kk = _heads((k * k_k).astype(f32))
    kk = kk * lax.rsqrt(jnp.maximum(jnp.sum(jnp.square(kk), axis=-1, keepdims=True), L2_EPS))
    k_a_h = _heads(k_a.astype(f32))
    y = jnp.zeros_like(rf)
    bonus = jnp.zeros_like(rf[..., :1])
    for d in range(2):
        w_lora = jnp.tanh(xw @ w1[d]) @ w2[d]
        decay = jnp.exp(-jnp.exp(-jax.nn.softplus(-(w0[d] + w_lora).astype(f32)) - 0.5))
        iclr = _heads(jax.nn.sigmoid((a0[d] + (xa @ a1[d]) @ a2[d]).astype(f32)))
        k_d = kf * (1.0 + (iclr - 1.0) * k_a_h)
        y = y + _wkv_scan(rf, _heads(decay), k_d, vf, -kk, kk * iclr, reverse=(d == 1))
        bonus = bonus + jnp.sum(rf * k_d * r_k, axis=-1, keepdims=True)
    mu = jnp.mean(y, axis=-1, keepdims=True)
    var = jnp.mean(jnp.square(y - mu), axis=-1, keepdims=True)
    y = ((y - mu) * lax.rsqrt(var + GN_EPS)).reshape(B, T, C) * lnx_g + lnx_b
    y = y + (bonus * vf).reshape(B, T, C)
    out = (y.astype(h.dtype) * g) @ w_o
    return out, v_first


def _column_pattern():
    q_col = np.arange(GRID_W).reshape(N_COL_BLOCKS, WIN_COLS)
    win_start = np.clip(q_col - WIN_COLS // 2, 0, GRID_W - WIN_COLS)
    strip_start = np.clip(np.arange(N_COL_BLOCKS) * WIN_COLS - WIN_COLS // 2, 0, GRID_W - STRIP_COLS)
    strip_cols = strip_start[:, None] + np.arange(STRIP_COLS)[None, :]
    key_col = strip_cols[:, None, :]
    col_mask = (key_col >= win_start[..., None]) & (key_col < win_start[..., None] + WIN_COLS)
    dc_idx = np.clip(key_col - q_col[..., None] + WIN_COLS - 1, 0, 2 * WIN_COLS - 2)
    return strip_cols, col_mask, dc_idx


def _neighbourhood_attention(h, w_qkv, q_g, k_g, rpb, w_o):
    B, T, C = h.shape
    rows = T // GRID_W
    kr = min(WIN_ROWS_MAX, rows)
    strip_cols, col_mask, dc_idx = _column_pattern()
    qkv = (h @ w_qkv).reshape(B, rows, GRID_W, 3, N_HEADS, HEAD_DIM)
    q = _rms_norm(qkv[:, :, :, 0], q_g) * (HEAD_DIM ** -0.5)
    k = _rms_norm(qkv[:, :, :, 1], k_g)
    v = qkv[:, :, :, 2]
    q, k, v = (jnp.transpose(z, (0, 3, 1, 2, 4)) for z in (q, k, v))
    row_off = jnp.arange(kr)

    def row_block(i):
        r0 = jnp.clip(i - kr // 2, 0, rows - kr)
        q_i = lax.dynamic_index_in_dim(q, i, axis=2, keepdims=False)
        q_i = q_i.reshape(B, N_HEADS, N_COL_BLOCKS, WIN_COLS, HEAD_DIM)
        k_s = lax.dynamic_slice_in_dim(k, r0, kr, axis=2)[:, :, :, strip_cols]
        v_s = lax.dynamic_slice_in_dim(v, r0, kr, axis=2)[:, :, :, strip_cols]
        s = jnp.einsum('bhcqd,bhrcsd->bhcqrs', q_i, k_s).astype(jnp.float32)
        dr_idx = r0 + row_off - i + WIN_ROWS_MAX - 1
        bias = jnp.take(rpb, dr_idx, axis=1)[:, :, dc_idx]
        s = s + jnp.transpose(bias, (0, 2, 3, 1, 4)).astype(jnp.float32)
        s = jnp.where(col_mask[:, :, None, :], s, -jnp.inf)
        p = jax.nn.softmax(s, axis=(-2, -1)).astype(v_s.dtype)
        o = jnp.einsum('bhcqrs,bhrcsd->bhcqd', p, v_s)
        return o.reshape(B, N_HEADS, GRID_W, HEAD_DIM)

    o = lax.map(row_block, jnp.arange(rows))
    o = jnp.transpose(o, (1, 0, 3, 2, 4)).reshape(B, T, C)
    return o @ w_o


def _trunk(x, p):
    v_first = None
    for layer in range(DEPTH):
        j = layer // N_MIXERS
        h = _rms_norm(x, p['norm_mix_g'][layer])
        if layer % N_MIXERS == 0:
            vres = None if j == 0 else (p['rw_v0'][j - 1], p['rw_v1'][j - 1], p['rw_v2'][j - 1])
            y, v_first = _rwkv7_mixer(
                h, v_first, vres, p['rw_mix'][j], p['rw_rkv'][j],
                p['rw_w0'][j], p['rw_w1'][j], p['rw_w2'][j],
                p['rw_a0'][j], p['rw_a1'][j], p['rw_a2'][j],
                p['rw_g1'][j], p['rw_g2'][j], p['rw_kk'][j], p['rw_ka'][j], p['rw_rk'][j],
                p['rw_lnx_g'][j], p['rw_lnx_b'][j], p['rw_o'][j])
        else:
            y = _neighbourhood_attention(h, p['na_qkv'][j], p['na_q_g'][j], p['na_k_g'][j],
                                         p['na_rpb'][j], p['na_o'][j])
        x = x + y
        x = x + _sq_relu_mlp(_rms_norm(x, p['norm_mlp_g'][layer]), p['w_up'][layer], p['w_down'][layer])
    return x


def _normal(k, shape, scale):
    return scale * jax.random.normal(k, shape, jnp.float32)


def setup_inputs(seed: int = 0) -> dict:
    key = jax.random.key(seed)
    k = jax.random.split(key, 32)
    C = D_MODEL
    inv = C ** -0.5
    NA_, NB_ = N_RWKV_LAYERS, N_NA_LAYERS
    return {
        'x_prompt': _normal(k[0], (BATCH, SEQ, C), 1.0),
        'x_sample': _normal(k[1], (DEC_BATCH, DEC_SEQ, C), 1.0),
        'norm_mix_g': 1.0 + _normal(k[2], (DEPTH, C), 0.05),
        'norm_mlp_g': 1.0 + _normal(k[3], (DEPTH, C), 0.05),
        'w_up': _normal(k[4], (DEPTH, C, D_FF), inv),
        'w_down': _normal(k[5], (DEPTH, D_FF, C), D_FF ** -0.5),
        'rw_mix': jax.random.uniform(k[6], (NA_, 6, C), jnp.float32),
        'rw_rkv': _normal(k[7], (NA_, 3, C, C), inv),
        'rw_w0': jax.random.uniform(k[8], (NA_, 2, C), jnp.float32, -6.0, 1.0),
        'rw_w1': _normal(k[9], (NA_, 2, C, D_DECAY_LORA), inv),
        'rw_w2': _normal(k[10], (NA_, 2, D_DECAY_LORA, C), 0.1 * D_DECAY_LORA ** -0.5),
        'rw_a0': _normal(k[11], (NA_, 2, C), 0.5),
        'rw_a1': _normal(k[12], (NA_, 2, C, D_ICLR_LORA), inv),
        'rw_a2': _normal(k[13], (NA_, 2, D_ICLR_LORA, C), D_ICLR_LORA ** -0.5),
        'rw_v0': _normal(k[14], (NA_ - 1, C), 0.5),
        'rw_v1': _normal(k[15], (NA_ - 1, C, D_VRES_LORA), inv),
        'rw_v2': _normal(k[16], (NA_ - 1, D_VRES_LORA, C), D_VRES_LORA ** -0.5),
        'rw_g1': _normal(k[17], (NA_, C, D_GATE_LORA), inv),
        'rw_g2': _normal(k[18], (NA_, D_GATE_LORA, C), D_GATE_LORA ** -0.5),
        'rw_kk': 0.85 + _normal(k[19], (NA_, C), 0.05),
        'rw_ka': 1.0 + _normal(k[20], (NA_, C), 0.05),
        'rw_rk': _normal(k[21], (NA_, N_HEADS, HEAD_DIM), 0.1),
        'rw_lnx_g': 1.0 + _normal(k[22], (NA_, C), 0.05),
        'rw_lnx_b': _normal(k[23], (NA_, C), 0.02),
        'rw_o': _normal(k[24], (NA_, C, C), inv),
        'na_qkv': _normal(k[25], (NB_, C, 3 * C), inv),
        'na_q_g': 1.0 + _normal(k[26], (NB_, HEAD_DIM), 0.05),
        'na_k_g': 1.0 + _normal(k[27], (NB_, HEAD_DIM), 0.05),
        'na_rpb': _normal(k[28], (NB_, N_HEADS, 2 * WIN_ROWS_MAX - 1, 2 * WIN_COLS - 1), 0.1),
        'na_o': _normal(k[29], (NB_, C, C), inv),
    }


def reference(x_prompt, x_sample, norm_mix_g, norm_mlp_g, w_up, w_down, rw_mix, rw_rkv, rw_w0, rw_w1, rw_w2,
              rw_a0, rw_a1, rw_a2, rw_v0, rw_v1, rw_v2, rw_g1, rw_g2, rw_kk, rw_ka, rw_rk, rw_lnx_g, rw_lnx_b,
              rw_o, na_qkv, na_q_g, na_k_g, na_rpb, na_o):
    p = dict(norm_mix_g=norm_mix_g, norm_mlp_g=norm_mlp_g, w_up=w_up, w_down=w_down,
             rw_mix=rw_mix, rw_rkv=rw_rkv, rw_w0=rw_w0, rw_w1=rw_w1, rw_w2=rw_w2,
             rw_a0=rw_a0, rw_a1=rw_a1, rw_a2=rw_a2, rw_v0=rw_v0, rw_v1=rw_v1, rw_v2=rw_v2,
             rw_g1=rw_g1, rw_g2=rw_g2, rw_kk=rw_kk, rw_ka=rw_ka, rw_rk=rw_rk,
             rw_lnx_g=rw_lnx_g, rw_lnx_b=rw_lnx_b, rw_o=rw_o,
             na_qkv=na_qkv, na_q_g=na_q_g, na_k_g=na_k_g, na_rpb=na_rpb, na_o=na_o)
    y_prompt = _trunk(x_prompt, p)
    y_sample = _trunk(x_sample, p)
    return (y_prompt, y_sample)
```

```python
import functools

import numpy as np
import jax
import jax.numpy as jnp
from jax import lax
from jax.experimental import pallas as pl
from jax.experimental.pallas import tpu as pltpu

F32 = jnp.float32
BF16 = jnp.bfloat16

HEAD_DIM = 64
PAIR = 2 * HEAD_DIM
SEG = 256
GRID_W = 64
WIN_ROWS = 8
WIN_COLS = 16
RMS_EPS = 1e-6
GN_EPS = 64e-5
L2_EPS = 1e-24
CHUNK = 64
INV_BASE = 8
NA_ROWS = 8
NEG_BIG = -1e30
DECAY_SCALE = float(np.exp(-0.5))
VMEM_LIMIT = 56 * 1024 * 1024


def _dot(a, b):
    return jnp.dot(a, b, preferred_element_type=F32)


def _dot_nt(a, b):
    return lax.dot_general(a, b, (((1,), (1,)), ((), ())), preferred_element_type=F32)


def _split2(x):
    hi = x.astype(BF16)
    lo = (x - hi.astype(F32)).astype(BF16)
    return hi, lo


def _split3(x):
    hi = x.astype(BF16)
    r1 = x - hi.astype(F32)
    mid = r1.astype(BF16)
    lo = (r1 - mid.astype(F32)).astype(BF16)
    return hi, mid, lo


def _segsum(x, e):
    outs = []
    for j in range(x.shape[1] // SEG):
        hi, lo = _split2(x[:, j * SEG:(j + 1) * SEG])
        outs.append(_dot(hi, e) + _dot(lo, e))
    return jnp.concatenate(outs, axis=1) if len(outs) > 1 else outs[0]


def _rms(x, g):
    return x * lax.rsqrt(jnp.mean(x * x, axis=-1, keepdims=True) + RMS_EPS) * g


def _const_spec(shape):
    nd = len(shape)
    return pl.BlockSpec(shape, lambda *_: (0,) * nd, pipeline_mode=pl.Buffered(1))


def _params(sem):
    return pltpu.CompilerParams(dimension_semantics=sem, vmem_limit_bytes=VMEM_LIMIT)


def _rwkv_pre_kernel(has_vres, seq_tiles, *refs):
    if has_vres:
        (x_ref, xp_ref, xn_ref, vf_ref, gn_ref, mix_ref, wrkv_ref, dn_ref, w2_ref, a2_ref, g2_ref,
         v2_ref, vec_ref, e_ref,
         r_o, v_o, kk_o, bv_o, g_o, lw0_o, lw1_o, kd0_o, kd1_o, b0_o, b1_o) = refs
    else:
        (x_ref, xp_ref, xn_ref, gn_ref, mix_ref, wrkv_ref, dn_ref, w2_ref, a2_ref, g2_ref,
         vec_ref, e_ref,
         r_o, v_o, kk_o, bv_o, g_o, lw0_o, lw1_o, kd0_o, kd1_o, b0_o, b1_o) = refs
    i = pl.program_id(0)
    tm, c = x_ref.shape
    gn = gn_ref[...]
    h = _rms(x_ref[...], gn)
    hp = _rms(xp_ref[7:8, :], gn)
    hn = _rms(xn_ref[0:1, :], gn)
    it = i % seq_tiles
    hp = jnp.where(it == 0, 0.0, hp)
    hn = jnp.where(it == seq_tiles - 1, 0.0, hn)
    row = lax.broadcasted_iota(jnp.int32, (tm, c), 0)
    h_prev = jnp.where(row == 0, hp, pltpu.roll(h, 1, 0))
    h_next = jnp.where(row == tm - 1, hn, pltpu.roll(h, tm - 1, 0))
    xx = 0.5 * (h_prev + h_next) - h

    def mixed(j):
        return (h + xx * mix_ref[j:j + 1, :]).astype(BF16)

    xv = mixed(2)
    r = _dot(mixed(0), wrkv_ref[0])
    k = _dot(mixed(1), wrkv_ref[1])
    v = _dot(xv, wrkv_ref[2])
    wl = jnp.tanh(_dot(mixed(3), dn_ref[:, 0:128])).astype(BF16)
    al = _dot(mixed(4), dn_ref[:, 128:256]).astype(BF16)
    gl = jax.nn.sigmoid(_dot(mixed(5), dn_ref[:, 256:512])).astype(BF16)
    g_o[...] = _dot(gl, g2_ref[...])
    if has_vres:
        vl = _dot(xv, dn_ref[:, 512:640]).astype(BF16)
        gate = jax.nn.sigmoid(vec_ref[8:9, :] + _dot(vl, v2_ref[...]))
        v = v + (vf_ref[...] - v) * gate
    v_o[...] = v
    r_o[...] = r
    e = e_ref[...]
    kk = k * vec_ref[4:5, :]
    kk = kk * lax.rsqrt(jnp.maximum(_segsum(kk * kk, e), L2_EPS))
    kk_o[...] = kk
    k_a = vec_ref[5:6, :]
    r_k = vec_ref[6:7, :]
    w_up = _dot(wl, w2_ref[...])
    a_up = _dot(al, a2_ref[...])
    bonus_arg = None
    for d, (lw_o, kd_o, b_o) in enumerate(((lw0_o, kd0_o, b0_o), (lw1_o, kd1_o, b1_o))):
        z = vec_ref[d:d + 1, :] + w_up[:, d * c:(d + 1) * c]
        lw_o[...] = -DECAY_SCALE * jax.nn.sigmoid(z)
        iclr = jax.nn.sigmoid(vec_ref[2 + d:3 + d, :] + a_up[:, d * c:(d + 1) * c])
        k_d = k * (1.0 + (iclr - 1.0) * k_a)
        kd_o[...] = k_d
        b_o[...] = kk * iclr
        term = r * k_d * r_k
        bonus_arg = term if bonus_arg is None else bonus_arg + term
    bv_o[...] = _segsum(bonus_arg, e) * v


def _rwkv_pre(x, v_first, p, seq_len, tm):
    n, c = x.shape
    has_vres = v_first is not None
    seq_tiles = seq_len // tm
    row_spec = pl.BlockSpec((tm, c), lambda i: (i, 0))
    nblk8 = n // 8
    prev_spec = pl.BlockSpec((8, c), lambda i: (jnp.maximum(i * (tm // 8) - 1, 0), 0))
    next_spec = pl.BlockSpec((8, c), lambda i: (jnp.minimum((i + 1) * (tm // 8), nblk8 - 1), 0))
    ins = [x, x, x]
    specs = [row_spec, prev_spec, next_spec]
    if has_vres:
        ins.append(v_first)
        specs.append(row_spec)
    consts = [p['gn'], p['mix'], p['wrkv'], p['dn'], p['w2'], p['a2'], p['g2']]
    if has_vres:
        consts.append(p['v2'])
    consts += [p['vec'], p['e']]
    ins += consts
    specs += [_const_spec(a.shape) for a in consts]
    out_shape = [jax.ShapeDtypeStruct((n, c), F32)] * 11
    return pl.pallas_call(
        functools.partial(_rwkv_pre_kernel, has_vres, seq_tiles),
        grid=(n // tm,),
        in_specs=specs,
        out_specs=[row_spec] * 11,
        out_shape=out_shape,
        compiler_params=_params(("parallel",)),
        name="rwkv_pre",
    )(*ins)


def _wkv_chunk(r, v, kk, lw, kd, b, s, reverse):
    L = CHUNK
    row = lax.broadcasted_iota(jnp.int32, (L, PAIR), 0)
    lane = lax.broadcasted_iota(jnp.int32, (L, PAIR), 1)
    col = lane & (HEAD_DIM - 1)
    head0 = lane < HEAD_DIM
    if reverse:
        strict, incl = col > row, col >= row
    else:
        strict, incl = col < row, col <= row

    def bd(q):
        return jnp.concatenate([jnp.where(head0, q, 0.0), jnp.where(head0, 0.0, q)], axis=0).astype(BF16)

    ti = lax.broadcasted_iota(jnp.int32, (L, L), 0)
    tj = lax.broadcasted_iota(jnp.int32, (L, L), 1)
    tri = jnp.where((tj >= ti) if reverse else (tj <= ti), 1.0, 0.0).astype(BF16)
    hi, mid, lo = _split3(lw)
    lc = _dot(tri, hi) + _dot(tri, mid) + _dot(tri, lo)
    last = 0 if reverse else L - 1
    lc_end = lc[last:last + 1, :]
    e_inc = jnp.exp(lc)
    e_exc = jnp.exp(lc - lw)
    e_inv = jnp.exp(-lc)
    e_rem = jnp.exp(lc_end - lc)
    a_t = -(kk * e_exc)
    r_t = r * e_inc
    b_t = b * e_inv
    k_t = kd * e_inv

    x = jnp.concatenate([a_t, r_t], axis=0).astype(BF16)
    yt = jnp.concatenate([bd(b_t), bd(k_t)], axis=0)
    sc = _dot_nt(x, yt)
    ab = jnp.where(strict, sc[:L, :PAIR], 0.0)
    ak = jnp.where(strict, sc[:L, PAIR:], 0.0)
    rb = jnp.where(incl, sc[L:, :PAIR], 0.0)
    rk = jnp.where(incl, sc[L:, PAIR:], 0.0)

    def same_block(n):
        return (col // n) == (row // n)

    a0 = jnp.where(same_block(INV_BASE), ab, 0.0)
    t = jnp.where(col == row, 1.0, 0.0) + a0
    pw = a0
    n = 2
    while n < INV_BASE:
        pw = _dot(pw.astype(BF16), bd(pw))
        t = t + _dot(t.astype(BF16), bd(pw))
        n *= 2
    size = INV_BASE
    while size < L:
        hi_r = (row & (2 * size - 1)) >= size
        hi_c = (col & (2 * size - 1)) >= size
        off = same_block(2 * size) & ((hi_c & ~hi_r) if reverse else (hi_r & ~hi_c))
        ta = _dot(t.astype(BF16), bd(jnp.where(off, ab, 0.0)))
        t = t + _dot(ta.astype(BF16), bd(t))
        size *= 2

    xs = _dot(x, s.astype(BF16))
    xv = _dot(jnp.concatenate([ak, rk], axis=0).astype(BF16), bd(v))
    u = _dot(t.astype(BF16), bd(xs[:L] + xv[:L]))
    y = xs[L:] + xv[L:] + _dot(rb.astype(BF16), bd(u))

    z = jnp.concatenate([b * e_rem, kd * e_rem], axis=0)
    uv = jnp.concatenate([u, v], axis=0).astype(BF16)
    upd = _dot(z.T.astype(BF16), uv)
    krow = lax.broadcasted_iota(jnp.int32, (PAIR, PAIR), 0)
    vcol = lax.broadcasted_iota(jnp.int32, (PAIR, PAIR), 1)
    same_head = (krow // HEAD_DIM) == (vcol // HEAD_DIM)
    decay_col = jnp.broadcast_to(jnp.exp(lc_end), (PAIR, PAIR)).T
    s_new = s * decay_col + jnp.where(same_head, upd, 0.0)
    return y, s_new


def _wkv_kernel(npairs, *refs):
    fwd = refs[0:6]
    bwd = refs[6:12]
    y_f, y_b, s_f, s_b = refs[12:16]

    @pl.when(pl.program_id(2) == 0)
    def _():
        s_f[...] = jnp.zeros_like(s_f)
        s_b[...] = jnp.zeros_like(s_b)

    for ins, y_ref, s_ref, reverse in ((fwd, y_f, s_f, False), (bwd, y_b, s_b, True)):
        for p in range(npairs):
            sl = slice(p * PAIR, (p + 1) * PAIR)
            args = [ref[:, sl] for ref in ins]
            y, s_new = _wkv_chunk(*args, s_ref[p], reverse)
            y_ref[:, sl] = y
            s_ref[p] = s_new


def _wkv(r, v, kk, lw0, lw1, kd0, kd1, b0, b1, batch, seq_len, npairs):
    n, c = r.shape
    nc = seq_len // CHUNK
    width = npairs * PAIR
    fwd_spec = pl.BlockSpec((CHUNK, width), lambda bi, hg, ci: (bi * nc + ci, hg))
    bwd_spec = pl.BlockSpec((CHUNK, width), lambda bi, hg, ci: (bi * nc + nc - 1 - ci, hg))
    ins = [r, v, kk, lw0, kd0, b0, r, v, kk, lw1, kd1, b1]
    return pl.pallas_call(
        functools.partial(_wkv_kernel, npairs),
        grid=(batch, c // width, nc),
        in_specs=[fwd_spec] * 6 + [bwd_spec] * 6,
        out_specs=[fwd_spec, bwd_spec],
        out_shape=[jax.ShapeDtypeStruct((n, c), F32)] * 2,
        scratch_shapes=[pltpu.VMEM((npairs, PAIR, PAIR), F32)] * 2,
        compiler_params=_params(("parallel", "parallel", "arbitrary")),
        name="wkv_scan",
    )(*ins)


def _mlp_tail(x1, gm_ref, wup_ref, wdn_ref, out_ref):
    hb = _rms(x1, gm_ref[...]).astype(BF16)
    c = x1.shape[1]
    acc = x1
    for j in range(wup_ref.shape[1] // c):
        u = jnp.maximum(_dot(hb, wup_ref[:, j * c:(j + 1) * c]), 0.0)
        acc = acc + _dot((u * u).astype(BF16), wdn_ref[j * c:(j + 1) * c, :])
    out_ref[...] = acc


def _na_post_kernel(x_ref, o_ref, wo_ref, gm_ref, wup_ref, wdn_ref, out_ref):
    x1 = x_ref[...] + _dot(o_ref[...], wo_ref[...])
    _mlp_tail(x1, gm_ref, wup_ref, wdn_ref, out_ref)


def _rwkv_post_kernel(x_ref, yf_ref, yb_ref, bv_ref, g_ref, ln_ref, e_ref, wo_ref, gm_ref, wup_ref,
                      wdn_ref, out_ref):
    e = e_ref[...]
    y = yf_ref[...] + yb_ref[...]
    inv_n = 1.0 / HEAD_DIM
    d = y - _segsum(y, e) * inv_n
    var = _segsum(d * d, e) * inv_n
    yn = d * lax.rsqrt(var + GN_EPS) * ln_ref[0:1, :] + ln_ref[1:2, :] + bv_ref[...]
    x1 = x_ref[...] + _dot((yn * g_ref[...]).astype(BF16), wo_ref[...])
    _mlp_tail(x1, gm_ref, wup_ref, wdn_ref, out_ref)


def _post(kernel, x, acts, consts, tm):
    n, c = x.shape
    row_spec = pl.BlockSpec((tm, c), lambda i: (i, 0))
    ins = [x] + list(acts) + list(consts)
    specs = [row_spec] * (1 + len(acts)) + [_const_spec(a.shape) for a in consts]
    return pl.pallas_call(
        kernel,
        grid=(n // tm,),
        in_specs=specs,
        out_specs=row_spec,
        out_shape=jax.ShapeDtypeStruct((n, c), F32),
        compiler_params=_params(("parallel",)),
        name=kernel.__name__.strip("_"),
    )(*ins)


def _na_qkv_kernel(x_ref, gn_ref, w_ref, qg_ref, kg_ref, e_ref, q_o, k_o, v_o):
    hb = _rms(x_ref[...], gn_ref[...]).astype(BF16)
    c = x_ref.shape[1]
    e = e_ref[...]
    inv_n = 1.0 / HEAD_DIM
    for part, (out, gain) in enumerate(((q_o, qg_ref), (k_o, kg_ref), (v_o, None))):
        for j in range(c // SEG):
            lo = part * c + j * SEG
            acc = _dot(hb, w_ref[:, lo:lo + SEG])
            if gain is not None:
                ms = _segsum(acc * acc, e) * inv_n
                acc = acc * lax.rsqrt(ms + RMS_EPS) * gain[...]
            out[:, j * SEG:(j + 1) * SEG] = acc.astype(out.dtype)


def _na_qkv(x, p, tm):
    n, c = x.shape
    row_spec = pl.BlockSpec((tm, c), lambda i: (i, 0))
    consts = [p['gn'], p['wqkv'], p['qg'], p['kg'], p['e']]
    return pl.pallas_call(
        _na_qkv_kernel,
        grid=(n // tm,),
        in_specs=[row_spec] + [_const_spec(a.shape) for a in consts],
        out_specs=[row_spec] * 3,
        out_shape=[jax.ShapeDtypeStruct((n, c), BF16)] * 3,
        compiler_params=_params(("parallel",)),
        name="na_qkv",
    )(x, *consts)


def _na_kernel(rows, q_ref, kp_ref, kc_ref, kn_ref, vp_ref, vc_ref, vn_ref, bias_ref, o_ref, kw_ref, vw_ref):
    ib = pl.program_id(2)
    blk = NA_ROWS * GRID_W
    for j, (kr, vr) in enumerate(((kp_ref, vp_ref), (kc_ref, vc_ref), (kn_ref, vn_ref))):
        kw_ref[j * blk:(j + 1) * blk, :] = kr[...]
        vw_ref[j * blk:(j + 1) * blk, :] = vr[...]
    lane = lax.broadcasted_iota(jnp.int32, (GRID_W, PAIR), 1)
    head0 = lane < HEAD_DIM
    nkeys = WIN_ROWS * GRID_W
    for ii in range(NA_ROWS):
        i = ib * NA_ROWS + ii
        r0 = jnp.clip(i - WIN_ROWS // 2, 0, rows - WIN_ROWS)
        start = pl.multiple_of((r0 - ib * NA_ROWS + NA_ROWS) * GRID_W, GRID_W)
        d0 = r0 - i + WIN_ROWS - 1
        q = q_ref[ii * GRID_W:(ii + 1) * GRID_W, :]
        zero = jnp.zeros_like(q)
        qs = jnp.concatenate([jnp.where(head0, q, zero), jnp.where(head0, zero, q)], axis=0)
        s = _dot_nt(qs, kw_ref[pl.ds(start, nkeys), :]) + bias_ref[0, d0]
        m = jnp.max(s, axis=-1, keepdims=True)
        pr = jnp.exp(s - m)
        inv_l = 1.0 / jnp.sum(pr, axis=-1, keepdims=True)
        pv = _dot(pr.astype(BF16), vw_ref[pl.ds(start, nkeys), :]) * inv_l
        o = jnp.where(head0, pv[:GRID_W], pv[GRID_W:])
        o_ref[ii * GRID_W:(ii + 1) * GRID_W, :] = o.astype(o_ref.dtype)


def _na(q, k, v, bias, batch, seq_len):
    n, c = q.shape
    rows = seq_len // GRID_W
    nrb = rows // NA_ROWS
    blk = NA_ROWS * GRID_W

    def spec(off):
        return pl.BlockSpec((blk, PAIR), lambda p, bi, ib: (bi * nrb + jnp.clip(ib + off, 0, nrb - 1), p))

    bias_spec = pl.BlockSpec((1,) + bias.shape[1:], lambda p, bi, ib: (p, 0, 0, 0))
    return pl.pallas_call(
        functools.partial(_na_kernel, rows),
        grid=(c // PAIR, batch, nrb),
        in_specs=[spec(0), spec(-1), spec(0), spec(1), spec(-1), spec(0), spec(1), bias_spec],
        out_specs=spec(0),
        out_shape=jax.ShapeDtypeStruct((n, c), BF16),
        scratch_shapes=[pltpu.VMEM((3 * blk, PAIR), BF16)] * 2,
        compiler_params=_params(("parallel", "parallel", "arbitrary")),
        name="na_attn",
    )(q, k, k, k, v, v, v, bias)


def _na_bias(rpb):
    nh = rpb.shape[0]
    qc = np.arange(GRID_W)[:, None]
    kc = np.arange(GRID_W)[None, :]
    win_start = np.clip(qc - WIN_COLS // 2, 0, GRID_W - WIN_COLS)
    valid = (kc >= win_start) & (kc < win_start + WIN_COLS)
    dc = np.clip(kc - qc + WIN_COLS - 1, 0, 2 * WIN_COLS - 2)
    tab = jnp.where(valid[None, None], rpb[:, :, dc], NEG_BIG)
    dr = np.arange(WIN_ROWS)[:, None] + np.arange(WIN_ROWS)[None, :]
    win = tab[:, dr]
    win = jnp.transpose(win, (0, 1, 3, 2, 4)).reshape(nh, WIN_ROWS, GRID_W, WIN_ROWS * GRID_W)
    win = win.reshape(nh // 2, 2, WIN_ROWS, GRID_W, WIN_ROWS * GRID_W)
    return jnp.transpose(win, (0, 2, 1, 3, 4)).reshape(nh // 2, WIN_ROWS, PAIR, WIN_ROWS * GRID_W)


def _block_diag2(w):
    k, c = w.shape[1:]
    z = jnp.zeros((k, c), w.dtype)
    return jnp.concatenate([jnp.concatenate([w[0], z], 1), jnp.concatenate([z, w[1]], 1)], 0)


def _pad_to(w, axis, size):
    pad = [(0, 0)] * w.ndim
    pad[axis] = (0, size - w.shape[axis])
    return jnp.pad(w, pad)


def _seg_matrix():
    i = np.arange(SEG) // HEAD_DIM
    return jnp.asarray(i[:, None] == i[None, :], BF16)


def _pack_rwkv(j, w):
    c = w['rw_rkv'].shape[-1]
    p = {}
    p['mix'] = _pad_to(w['rw_mix'][j], 0, 8)
    p['wrkv'] = w['rw_rkv'][j].astype(BF16)
    dn = [w['rw_w1'][j, 0], w['rw_w1'][j, 1], w['rw_a1'][j, 0], w['rw_a1'][j, 1],
          _pad_to(w['rw_g1'][j], 1, 256)]
    vec = [w['rw_w0'][j, 0], w['rw_w0'][j, 1], w['rw_a0'][j, 0], w['rw_a0'][j, 1],
           w['rw_kk'][j], w['rw_ka'][j], w['rw_rk'][j].reshape(c), jnp.zeros((c,), F32)]
    if j > 0:
        dn.append(_pad_to(w['rw_v1'][j - 1], 1, 128))
        vec.append(w['rw_v0'][j - 1])
        p['v2'] = _pad_to(w['rw_v2'][j - 1], 0, 128).astype(BF16)
    p['dn'] = jnp.concatenate(dn, axis=1).astype(BF16)
    p['vec'] = _pad_to(jnp.stack(vec), 0, 16)
    p['w2'] = _block_diag2(w['rw_w2'][j]).astype(BF16)
    p['a2'] = _block_diag2(w['rw_a2'][j]).astype(BF16)
    p['g2'] = _pad_to(w['rw_g2'][j], 0, 256).astype(BF16)
    p['ln'] = _pad_to(jnp.stack([w['rw_lnx_g'][j], w['rw_lnx_b'][j]]), 0, 8)
    p['wo'] = w['rw_o'][j].astype(BF16)
    return p


def _pack_na(j, w):
    nh = w['na_rpb'].shape[1]
    scale = HEAD_DIM ** -0.5
    return {
        'wqkv': w['na_qkv'][j].astype(BF16),
        'qg': jnp.tile(w['na_q_g'][j] * scale, SEG // HEAD_DIM)[None, :],
        'kg': jnp.tile(w['na_k_g'][j], SEG // HEAD_DIM)[None, :],
        'bias': _na_bias(w['na_rpb'][j]),
        'wo': w['na_o'][j].astype(BF16),
    }


def _trunk(x3, w, depth):
    batch, seq_len, c = x3.shape
    x = x3.reshape(batch * seq_len, c)
    e = _seg_matrix()
    v_first = None
    for layer in range(depth):
        j = layer // 2
        gn = w['norm_mix_g'][layer][None, :]
        mlp = [w['norm_mlp_g'][layer][None, :], w['w_up'][layer].astype(BF16), w['w_down'][layer].astype(BF16)]
        if layer % 2 == 0:
            p = _pack_rwkv(j, w)
            p['gn'], p['e'] = gn, e
            r, v, kk, bv, g, lw0, lw1, kd0, kd1, b0, b1 = _rwkv_pre(x, v_first, p, seq_len, 256)
            if v_first is None:
                v_first = v
            y_f, y_b = _wkv(r, v, kk, lw0, lw1, kd0, kd1, b0, b1, batch, seq_len, 4)
            x = _post(_rwkv_post_kernel, x, [y_f, y_b, bv, g], [p['ln'], e, p['wo']] + mlp, 256)
        else:
            p = _pack_na(j, w)
            p['gn'], p['e'] = gn, e
            q, k, v = _na_qkv(x, p, 512)
            o = _na(q, k, v, p['bias'], batch, seq_len)
            x = _post(_na_post_kernel, x, [o], [p['wo']] + mlp, 512)
    return x.reshape(batch, seq_len, c)


def kernel(x_prompt, x_sample, norm_mix_g, norm_mlp_g, w_up, w_down, rw_mix, rw_rkv, rw_w0, rw_w1, rw_w2,
           rw_a0, rw_a1, rw_a2, rw_v0, rw_v1, rw_v2, rw_g1, rw_g2, rw_kk, rw_ka, rw_rk, rw_lnx_g, rw_lnx_b,
           rw_o, na_qkv, na_q_g, na_k_g, na_rpb, na_o):
    w = dict(norm_mix_g=norm_mix_g, norm_mlp_g=norm_mlp_g, w_up=w_up, w_down=w_down,
             rw_mix=rw_mix, rw_rkv=rw_rkv, rw_w0=rw_w0, rw_w1=rw_w1, rw_w2=rw_w2,
             rw_a0=rw_a0, rw_a1=rw_a1, rw_a2=rw_a2, rw_v0=rw_v0, rw_v1=rw_v1, rw_v2=rw_v2,
             rw_g1=rw_g1, rw_g2=rw_g2, rw_kk=rw_kk, rw_ka=rw_ka, rw_rk=rw_rk,
             rw_lnx_g=rw_lnx_g, rw_lnx_b=rw_lnx_b, rw_o=rw_o,
             na_qkv=na_qkv, na_q_g=na_q_g, na_k_g=na_k_g, na_rpb=na_rpb, na_o=na_o)
    depth = norm_mix_g.shape[0]
    return _trunk(x_prompt, w, depth), _trunk(x_sample, w, depth)
```

```python
import functools

import numpy as np
import jax
import jax.numpy as jnp
from jax import lax
from jax.experimental import pallas as pl
from jax.experimental.pallas import tpu as pltpu

F32 = jnp.float32
BF16 = jnp.bfloat16

HEAD_DIM = 64
PAIR = 2 * HEAD_DIM
SEG = 256
GRID_W = 64
WIN_ROWS = 8
WIN_COLS = 16
RMS_EPS = 1e-6
GN_EPS = 64e-5
L2_EPS = 1e-24
CHUNK = 64
INV_BASE = 8
WKV_PAIRS = 8
WKV_CHUNKS = 4
NA_ROWS = 8
NEG_BIG = -1e30
DECAY_SCALE = float(np.exp(-0.5))
VMEM_LIMIT = 56 * 1024 * 1024


def _dot(a, b):
    return jnp.dot(a, b, preferred_element_type=F32)


def _dot_nt(a, b):
    return lax.dot_general(a, b, (((1,), (1,)), ((), ())), preferred_element_type=F32)


def _split2(x):
    hi = x.astype(BF16)
    lo = (x - hi.astype(F32)).astype(BF16)
    return hi, lo


def _split3(x):
    hi = x.astype(BF16)
    r1 = x - hi.astype(F32)
    mid = r1.astype(BF16)
    lo = (r1 - mid.astype(F32)).astype(BF16)
    return hi, mid, lo


def _segsum(x, e):
    outs = []
    for j in range(x.shape[1] // SEG):
        hi, lo = _split2(x[:, j * SEG:(j + 1) * SEG])
        outs.append(_dot(hi, e) + _dot(lo, e))
    return jnp.concatenate(outs, axis=1) if len(outs) > 1 else outs[0]


def _rms(x, g):
    return x * lax.rsqrt(jnp.mean(x * x, axis=-1, keepdims=True) + RMS_EPS) * g


def _const_spec(shape):
    nd = len(shape)
    return pl.BlockSpec(shape, lambda *_: (0,) * nd, pipeline_mode=pl.Buffered(1))


def _params(sem):
    return pltpu.CompilerParams(dimension_semantics=sem, vmem_limit_bytes=VMEM_LIMIT)


def _rwkv_pre_kernel(has_vres, seq_tiles, *refs):
    if has_vres:
        (x_ref, xp_ref, xn_ref, vf_ref, gn_ref, mix_ref, wrkv_ref, dn_ref, w2_ref, a2_ref, g2_ref,
         v2_ref, vec_ref, e_ref,
         r_o, v_o, kk_o, bv_o, g_o, lw0_o, lw1_o, kd0_o, kd1_o, b0_o, b1_o) = refs
    else:
        (x_ref, xp_ref, xn_ref, gn_ref, mix_ref, wrkv_ref, dn_ref, w2_ref, a2_ref, g2_ref,
         vec_ref, e_ref,
         r_o, v_o, kk_o, bv_o, g_o, lw0_o, lw1_o, kd0_o, kd1_o, b0_o, b1_o) = refs
    i = pl.program_id(0)
    tm, c = x_ref.shape
    gn = gn_ref[...]
    h = _rms(x_ref[...], gn)
    hp = _rms(xp_ref[7:8, :], gn)
    hn = _rms(xn_ref[0:1, :], gn)
    it = i % seq_tiles
    hp = jnp.where(it == 0, 0.0, hp)
    hn = jnp.where(it == seq_tiles - 1, 0.0, hn)
    row = lax.broadcasted_iota(jnp.int32, (tm, c), 0)
    h_prev = jnp.where(row == 0, hp, pltpu.roll(h, 1, 0))
    h_next = jnp.where(row == tm - 1, hn, pltpu.roll(h, tm - 1, 0))
    xx = 0.5 * (h_prev + h_next) - h

    def mixed(j):
        return (h + xx * mix_ref[j:j + 1, :]).astype(BF16)

    xv = mixed(2)
    r = _dot(mixed(0), wrkv_ref[0])
    k = _dot(mixed(1), wrkv_ref[1])
    v = _dot(xv, wrkv_ref[2])
    wl = jnp.tanh(_dot(mixed(3), dn_ref[:, 0:128])).astype(BF16)
    al = _dot(mixed(4), dn_ref[:, 128:256]).astype(BF16)
    gl = jax.nn.sigmoid(_dot(mixed(5), dn_ref[:, 256:512])).astype(BF16)
    g_o[...] = _dot(gl, g2_ref[...])
    if has_vres:
        vl = _dot(xv, dn_ref[:, 512:640]).astype(BF16)
        gate = jax.nn.sigmoid(vec_ref[8:9, :] + _dot(vl, v2_ref[...]))
        v = v + (vf_ref[...] - v) * gate
    v_o[...] = v
    r_o[...] = r
    e = e_ref[...]
    kk = k * vec_ref[4:5, :]
    kk = kk * lax.rsqrt(jnp.maximum(_segsum(kk * kk, e), L2_EPS))
    kk_o[...] = kk
    k_a = vec_ref[5:6, :]
    r_k = vec_ref[6:7, :]
    w_up = _dot(wl, w2_ref[...])
    a_up = _dot(al, a2_ref[...])
    bonus_arg = None
    for d, (lw_o, kd_o, b_o) in enumerate(((lw0_o, kd0_o, b0_o), (lw1_o, kd1_o, b1_o))):
        z = vec_ref[d:d + 1, :] + w_up[:, d * c:(d + 1) * c]
        lw_o[...] = -DECAY_SCALE * jax.nn.sigmoid(z)
        iclr = jax.nn.sigmoid(vec_ref[2 + d:3 + d, :] + a_up[:, d * c:(d + 1) * c])
        k_d = k * (1.0 + (iclr - 1.0) * k_a)
        kd_o[...] = k_d
        b_o[...] = kk * iclr
        term = r * k_d * r_k
        bonus_arg = term if bonus_arg is None else bonus_arg + term
    bv_o[...] = _segsum(bonus_arg, e) * v


def _rwkv_pre(x, v_first, p, seq_len, tm):
    n, c = x.shape
    has_vres = v_first is not None
    seq_tiles = seq_len // tm
    row_spec = pl.BlockSpec((tm, c), lambda i: (i, 0))
    nblk8 = n // 8
    prev_spec = pl.BlockSpec((8, c), lambda i: (jnp.maximum(i * (tm // 8) - 1, 0), 0))
    next_spec = pl.BlockSpec((8, c), lambda i: (jnp.minimum((i + 1) * (tm // 8), nblk8 - 1), 0))
    ins = [x, x, x]
    specs = [row_spec, prev_spec, next_spec]
    if has_vres:
        ins.append(v_first)
        specs.append(row_spec)
    consts = [p['gn'], p['mix'], p['wrkv'], p['dn'], p['w2'], p['a2'], p['g2']]
    if has_vres:
        consts.append(p['v2'])
    consts += [p['vec'], p['e']]
    ins += consts
    specs += [_const_spec(a.shape) for a in consts]
    out_shape = [jax.ShapeDtypeStruct((n, c), F32)] * 11
    return pl.pallas_call(
        functools.partial(_rwkv_pre_kernel, has_vres, seq_tiles),
        grid=(n // tm,),
        in_specs=specs,
        out_specs=[row_spec] * 11,
        out_shape=out_shape,
        compiler_params=_params(("parallel",)),
        name="rwkv_pre",
    )(*ins)


def _wkv_masks(reverse):
    L = CHUNK
    row = lax.broadcasted_iota(jnp.int32, (L, PAIR), 0)
    lane = lax.broadcasted_iota(jnp.int32, (L, PAIR), 1)
    col = lane & (HEAD_DIM - 1)

    def same_block(n):
        return (col // n) == (row // n)

    offs = []
    size = INV_BASE
    while size < L:
        hi_r = (row & (2 * size - 1)) >= size
        hi_c = (col & (2 * size - 1)) >= size
        offs.append(same_block(2 * size) & ((hi_c & ~hi_r) if reverse else (hi_r & ~hi_c)))
        size *= 2
    ti = lax.broadcasted_iota(jnp.int32, (L, L), 0)
    tj = lax.broadcasted_iota(jnp.int32, (L, L), 1)
    krow = lax.broadcasted_iota(jnp.int32, (PAIR, PAIR), 0)
    vcol = lax.broadcasted_iota(jnp.int32, (PAIR, PAIR), 1)
    return dict(
        head0=lane < HEAD_DIM,
        strict=(col > row) if reverse else (col < row),
        incl=(col >= row) if reverse else (col <= row),
        eye=jnp.where(col == row, 1.0, 0.0),
        base=same_block(INV_BASE),
        offs=offs,
        tri=jnp.where((tj >= ti) if reverse else (tj <= ti), 1.0, 0.0).astype(BF16),
        same_head=(krow // HEAD_DIM) == (vcol // HEAD_DIM),
        last=0 if reverse else L - 1,
    )


def _bd(q, head0):
    return jnp.concatenate([jnp.where(head0, q, 0.0), jnp.where(head0, 0.0, q)], axis=0).astype(BF16)


def _wkv_prepare(probs, m):
    L = CHUNK
    head0 = m['head0']
    lcs = []
    for (_, _, _, lw, _, _) in probs:
        hi, mid, lo = _split3(lw)
        lcs.append(_dot(m['tri'], hi) + _dot(m['tri'], mid) + _dot(m['tri'], lo))
    xs, yts, zts, dcs = [], [], [], []
    for (r, v, kk, lw, kd, b), lc in zip(probs, lcs):
        lc_end = lc[m['last']:m['last'] + 1, :]
        e_inv = jnp.exp(-lc)
        e_rem = jnp.exp(lc_end - lc)
        a_t = -(kk * jnp.exp(lc - lw))
        r_t = r * jnp.exp(lc)
        xs.append(jnp.concatenate([a_t, r_t], axis=0).astype(BF16))
        yts.append(jnp.concatenate([_bd(b * e_inv, head0), _bd(kd * e_inv, head0)], axis=0))
        z = jnp.concatenate([b * e_rem, kd * e_rem], axis=0)
        zts.append(z.T.astype(BF16))
        dcs.append(jnp.broadcast_to(jnp.exp(lc_end), (PAIR, PAIR)).T)
    scs = [_dot_nt(x, yt) for x, yt in zip(xs, yts)]
    abs_ = [jnp.where(m['strict'], sc[:L, :PAIR], 0.0) for sc in scs]
    akrk = [jnp.concatenate([jnp.where(m['strict'], sc[:L, PAIR:], 0.0),
                             jnp.where(m['incl'], sc[L:, PAIR:], 0.0)], axis=0).astype(BF16) for sc in scs]
    rbs = [jnp.where(m['incl'], sc[L:, :PAIR], 0.0).astype(BF16) for sc in scs]
    xvs = [_dot(a, _bd(p[1], head0)) for a, p in zip(akrk, probs)]

    pws = [jnp.where(m['base'], ab, 0.0) for ab in abs_]
    ts = [m['eye'] + pw for pw in pws]
    n = 2
    while n < INV_BASE:
        pws = [_dot(pw.astype(BF16), _bd(pw, head0)) for pw in pws]
        ts = [t + _dot(t.astype(BF16), _bd(pw, head0)) for t, pw in zip(ts, pws)]
        n *= 2
    for off in m['offs']:
        tas = [_dot(t.astype(BF16), _bd(jnp.where(off, ab, 0.0), head0)) for t, ab in zip(ts, abs_)]
        ts = [t + _dot(ta.astype(BF16), _bd(t, head0)) for t, ta in zip(ts, tas)]
    return [dict(x=x, t=t.astype(BF16), xv=xv, rb=rb, zt=zt, dc=dc, v=p[1])
            for x, t, xv, rb, zt, dc, p in zip(xs, ts, xvs, rbs, zts, dcs, probs)]


def _wkv_advance(preps, states, m):
    L = CHUNK
    head0 = m['head0']
    xss = [_dot(q['x'], s.astype(BF16)) for q, s in zip(preps, states)]
    us = [_dot(q['t'], _bd(xs[:L] + q['xv'][:L], head0)) for q, xs in zip(preps, xss)]
    ys = [xs[L:] + q['xv'][L:] + _dot(q['rb'], _bd(u, head0)) for q, xs, u in zip(preps, xss, us)]
    upds = [_dot(q['zt'], jnp.concatenate([u, q['v']], axis=0).astype(BF16)) for q, u in zip(preps, us)]
    new = [s * q['dc'] + jnp.where(m['same_head'], upd, 0.0) for q, s, upd in zip(preps, states, upds)]
    return ys, new


def _wkv_kernel(npairs, nchunks, *refs):
    fwd = refs[0:6]
    bwd = refs[6:12]
    y_f, y_b, s_f, s_b = refs[12:16]

    @pl.when(pl.program_id(2) == 0)
    def _():
        s_f[...] = jnp.zeros_like(s_f)
        s_b[...] = jnp.zeros_like(s_b)

    for ins, y_ref, s_ref, reverse in ((fwd, y_f, s_f, False), (bwd, y_b, s_b, True)):
        m = _wkv_masks(reverse)
        probs = []
        for ck in range(nchunks):
            for p in range(npairs):
                probs.append(tuple(ref[ck * CHUNK:(ck + 1) * CHUNK, p * PAIR:(p + 1) * PAIR] for ref in ins))
        preps = _wkv_prepare(probs, m)
        states = [s_ref[p] for p in range(npairs)]
        for ck in (reversed(range(nchunks)) if reverse else range(nchunks)):
            ys, states = _wkv_advance(preps[ck * npairs:(ck + 1) * npairs], states, m)
            for p, y in enumerate(ys):
                y_ref[ck * CHUNK:(ck + 1) * CHUNK, p * PAIR:(p + 1) * PAIR] = y
        for p, s in enumerate(states):
            s_ref[p] = s


def _wkv(r, v, kk, lw0, lw1, kd0, kd1, b0, b1, batch, seq_len, npairs, nchunks):
    n, c = r.shape
    rows = nchunks * CHUNK
    nb = seq_len // rows
    width = npairs * PAIR
    fwd_spec = pl.BlockSpec((rows, width), lambda bi, hg, ci: (bi * nb + ci, hg))
    bwd_spec = pl.BlockSpec((rows, width), lambda bi, hg, ci: (bi * nb + nb - 1 - ci, hg))
    ins = [r, v, kk, lw0, kd0, b0, r, v, kk, lw1, kd1, b1]
    return pl.pallas_call(
        functools.partial(_wkv_kernel, npairs, nchunks),
        grid=(batch, c // width, nb),
        in_specs=[fwd_spec] * 6 + [bwd_spec] * 6,
        out_specs=[fwd_spec, bwd_spec],
        out_shape=[jax.ShapeDtypeStruct((n, c), F32)] * 2,
        scratch_shapes=[pltpu.VMEM((npairs, PAIR, PAIR), F32)] * 2,
        compiler_params=_params(("parallel", "parallel", "arbitrary")),
        name="wkv_scan",
    )(*ins)


def _mlp_tail(x1, gm_ref, wup_ref, wdn_ref, out_ref):
    hb = _rms(x1, gm_ref[...]).astype(BF16)
    c = x1.shape[1]
    acc = x1
    for j in range(wup_ref.shape[1] // c):
        u = jnp.maximum(_dot(hb, wup_ref[:, j * c:(j + 1) * c]), 0.0)
        acc = acc + _dot((u * u).astype(BF16), wdn_ref[j * c:(j + 1) * c, :])
    out_ref[...] = acc


def _na_post_kernel(x_ref, o_ref, wo_ref, gm_ref, wup_ref, wdn_ref, out_ref):
    x1 = x_ref[...] + _dot(o_ref[...], wo_ref[...])
    _mlp_tail(x1, gm_ref, wup_ref, wdn_ref, out_ref)


def _rwkv_post_kernel(x_ref, yf_ref, yb_ref, bv_ref, g_ref, ln_ref, e_ref, wo_ref, gm_ref, wup_ref,
                      wdn_ref, out_ref):
    e = e_ref[...]
    y = yf_ref[...] + yb_ref[...]
    inv_n = 1.0 / HEAD_DIM
    d = y - _segsum(y, e) * inv_n
    var = _segsum(d * d, e) * inv_n
    yn = d * lax.rsqrt(var + GN_EPS) * ln_ref[0:1, :] + ln_ref[1:2, :] + bv_ref[...]
    x1 = x_ref[...] + _dot((yn * g_ref[...]).astype(BF16), wo_ref[...])
    _mlp_tail(x1, gm_ref, wup_ref, wdn_ref, out_ref)


def _post(kernel, x, acts, consts, tm):
    n, c = x.shape
    row_spec = pl.BlockSpec((tm, c), lambda i: (i, 0))
    ins = [x] + list(acts) + list(consts)
    specs = [row_spec] * (1 + len(acts)) + [_const_spec(a.shape) for a in consts]
    return pl.pallas_call(
        kernel,
        grid=(n // tm,),
        in_specs=specs,
        out_specs=row_spec,
        out_shape=jax.ShapeDtypeStruct((n, c), F32),
        compiler_params=_params(("parallel",)),
        name=kernel.__name__.strip("_"),
    )(*ins)


def _na_qkv_kernel(x_ref, gn_ref, w_ref, qg_ref, kg_ref, e_ref, q_o, k_o, v_o):
    hb = _rms(x_ref[...], gn_ref[...]).astype(BF16)
    c = x_ref.shape[1]
    e = e_ref[...]
    inv_n = 1.0 / HEAD_DIM
    for part, (out, gain) in enumerate(((q_o, qg_ref), (k_o, kg_ref), (v_o, None))):
        for j in range(c // SEG):
            lo = part * c + j * SEG
            acc = _dot(hb, w_ref[:, lo:lo + SEG])
            if gain is not None:
                ms = _segsum(acc * acc, e) * inv_n
                acc = acc * lax.rsqrt(ms + RMS_EPS) * gain[...]
            out[:, j * SEG:(j + 1) * SEG] = acc.astype(out.dtype)


def _na_qkv(x, p, tm):
    n, c = x.shape
    row_spec = pl.BlockSpec((tm, c), lambda i: (i, 0))
    consts = [p['gn'], p['wqkv'], p['qg'], p['kg'], p['e']]
    return pl.pallas_call(
        _na_qkv_kernel,
        grid=(n // tm,),
        in_specs=[row_spec] + [_const_spec(a.shape) for a in consts],
        out_specs=[row_spec] * 3,
        out_shape=[jax.ShapeDtypeStruct((n, c), BF16)] * 3,
        compiler_params=_params(("parallel",)),
        name="na_qkv",
    )(x, *consts)


def _na_kernel(rows, q_ref, kp_ref, kc_ref, kn_ref, vp_ref, vc_ref, vn_ref, bias_ref, o_ref, kw_ref, vw_ref):
    ib = pl.program_id(2)
    blk = NA_ROWS * GRID_W
    for j, (kr, vr) in enumerate(((kp_ref, vp_ref), (kc_ref, vc_ref), (kn_ref, vn_ref))):
        kw_ref[j * blk:(j + 1) * blk, :] = kr[...]
        vw_ref[j * blk:(j + 1) * blk, :] = vr[...]
    lane = lax.broadcasted_iota(jnp.int32, (GRID_W, PAIR), 1)
    head0 = lane < HEAD_DIM
    nkeys = WIN_ROWS * GRID_W
    for ii in range(NA_ROWS):
        i = ib * NA_ROWS + ii
        r0 = jnp.clip(i - WIN_ROWS // 2, 0, rows - WIN_ROWS)
        start = pl.multiple_of((r0 - ib * NA_ROWS + NA_ROWS) * GRID_W, GRID_W)
        d0 = r0 - i + WIN_ROWS - 1
        q = q_ref[ii * GRID_W:(ii + 1) * GRID_W, :]
        zero = jnp.zeros_like(q)
        qs = jnp.concatenate([jnp.where(head0, q, zero), jnp.where(head0, zero, q)], axis=0)
        s = _dot_nt(qs, kw_ref[pl.ds(start, nkeys), :]) + bias_ref[0, d0]
        m = jnp.max(s, axis=-1, keepdims=True)
        pr = jnp.exp(s - m)
        inv_l = 1.0 / jnp.sum(pr, axis=-1, keepdims=True)
        pv = _dot(pr.astype(BF16), vw_ref[pl.ds(start, nkeys), :]) * inv_l
        o = jnp.where(head0, pv[:GRID_W], pv[GRID_W:])
        o_ref[ii * GRID_W:(ii + 1) * GRID_W, :] = o.astype(o_ref.dtype)


def _na(q, k, v, bias, batch, seq_len):
    n, c = q.shape
    rows = seq_len // GRID_W
    nrb = rows // NA_ROWS
    blk = NA_ROWS * GRID_W

    def spec(off):
        return pl.BlockSpec((blk, PAIR), lambda p, bi, ib: (bi * nrb + jnp.clip(ib + off, 0, nrb - 1), p))

    bias_spec = pl.BlockSpec((1,) + bias.shape[1:], lambda p, bi, ib: (p, 0, 0, 0))
    return pl.pallas_call(
        functools.partial(_na_kernel, rows),
        grid=(c // PAIR, batch, nrb),
        in_specs=[spec(0), spec(-1), spec(0), spec(1), spec(-1), spec(0), spec(1), bias_spec],
        out_specs=spec(0),
        out_shape=jax.ShapeDtypeStruct((n, c), BF16),
        scratch_shapes=[pltpu.VMEM((3 * blk, PAIR), BF16)] * 2,
        compiler_params=_params(("parallel", "parallel", "arbitrary")),
        name="na_attn",
    )(q, k, k, k, v, v, v, bias)


def _na_bias(rpb):
    nh = rpb.shape[0]
    qc = np.arange(GRID_W)[:, None]
    kc = np.arange(GRID_W)[None, :]
    win_start = np.clip(qc - WIN_COLS // 2, 0, GRID_W - WIN_COLS)
    valid = (kc >= win_start) & (kc < win_start + WIN_COLS)
    dc = np.clip(kc - qc + WIN_COLS - 1, 0, 2 * WIN_COLS - 2)
    tab = jnp.where(valid[None, None], rpb[:, :, dc], NEG_BIG)
    dr = np.arange(WIN_ROWS)[:, None] + np.arange(WIN_ROWS)[None, :]
    win = tab[:, dr]
    win = jnp.transpose(win, (0, 1, 3, 2, 4)).reshape(nh, WIN_ROWS, GRID_W, WIN_ROWS * GRID_W)
    win = win.reshape(nh // 2, 2, WIN_ROWS, GRID_W, WIN_ROWS * GRID_W)
    return jnp.transpose(win, (0, 2, 1, 3, 4)).reshape(nh // 2, WIN_ROWS, PAIR, WIN_ROWS * GRID_W)


def _block_diag2(w):
    k, c = w.shape[1:]
    z = jnp.zeros((k, c), w.dtype)
    return jnp.concatenate([jnp.concatenate([w[0], z], 1), jnp.concatenate([z, w[1]], 1)], 0)


def _pad_to(w, axis, size):
    pad = [(0, 0)] * w.ndim
    pad[axis] = (0, size - w.shape[axis])
    return jnp.pad(w, pad)


def _seg_matrix():
    i = np.arange(SEG) // HEAD_DIM
    return jnp.asarray(i[:, None] == i[None, :], BF16)


def _pack_rwkv(j, w):
    c = w['rw_rkv'].shape[-1]
    p = {}
    p['mix'] = _pad_to(w['rw_mix'][j], 0, 8)
    p['wrkv'] = w['rw_rkv'][j].astype(BF16)
    dn = [w['rw_w1'][j, 0], w['rw_w1'][j, 1], w['rw_a1'][j, 0], w['rw_a1'][j, 1],
          _pad_to(w['rw_g1'][j], 1, 256)]
    vec = [w['rw_w0'][j, 0], w['rw_w0'][j, 1], w['rw_a0'][j, 0], w['rw_a0'][j, 1],
           w['rw_kk'][j], w['rw_ka'][j], w['rw_rk'][j].reshape(c), jnp.zeros((c,), F32)]
    if j > 0:
        dn.append(_pad_to(w['rw_v1'][j - 1], 1, 128))
        vec.append(w['rw_v0'][j - 1])
        p['v2'] = _pad_to(w['rw_v2'][j - 1], 0, 128).astype(BF16)
    p['dn'] = jnp.concatenate(dn, axis=1).astype(BF16)
    p['vec'] = _pad_to(jnp.stack(vec), 0, 16)
    p['w2'] = _block_diag2(w['rw_w2'][j]).astype(BF16)
    p['a2'] = _block_diag2(w['rw_a2'][j]).astype(BF16)
    p['g2'] = _pad_to(w['rw_g2'][j], 0, 256).astype(BF16)
    p['ln'] = _pad_to(jnp.stack([w['rw_lnx_g'][j], w['rw_lnx_b'][j]]), 0, 8)
    p['wo'] = w['rw_o'][j].astype(BF16)
    return p


def _pack_na(j, w):
    scale = HEAD_DIM ** -0.5
    return {
        'wqkv': w['na_qkv'][j].astype(BF16),
        'qg': jnp.tile(w['na_q_g'][j] * scale, SEG // HEAD_DIM)[None, :],
        'kg': jnp.tile(w['na_k_g'][j], SEG // HEAD_DIM)[None, :],
        'bias': _na_bias(w['na_rpb'][j]),
        'wo': w['na_o'][j].astype(BF16),
    }


def _trunk(x3, w, depth):
    batch, seq_len, c = x3.shape
    x = x3.reshape(batch * seq_len, c)
    e = _seg_matrix()
    v_first = None
    for layer in range(depth):
        j = layer // 2
        gn = w['norm_mix_g'][layer][None, :]
        mlp = [w['norm_mlp_g'][layer][None, :], w['w_up'][layer].astype(BF16), w['w_down'][layer].astype(BF16)]
        if layer % 2 == 0:
            p = _pack_rwkv(j, w)
            p['gn'], p['e'] = gn, e
            r, v, kk, bv, g, lw0, lw1, kd0, kd1, b0, b1 = _rwkv_pre(x, v_first, p, seq_len, 256)
            if v_first is None:
                v_first = v
            y_f, y_b = _wkv(r, v, kk, lw0, lw1, kd0, kd1, b0, b1, batch, seq_len, WKV_PAIRS, WKV_CHUNKS)
            x = _post(_rwkv_post_kernel, x, [y_f, y_b, bv, g], [p['ln'], e, p['wo']] + mlp, 256)
        else:
            p = _pack_na(j, w)
            p['gn'], p['e'] = gn, e
            q, k, v = _na_qkv(x, p, 512)
            o = _na(q, k, v, p['bias'], batch, seq_len)
            x = _post(_na_post_kernel, x, [o], [p['wo']] + mlp, 512)
    return x.reshape(batch, seq_len, c)


def kernel(x_prompt, x_sample, norm_mix_g, norm_mlp_g, w_up, w_down, rw_mix, rw_rkv, rw_w0, rw_w1, rw_w2,
           rw_a0, rw_a1, rw_a2, rw_v0, rw_v1, rw_v2, rw_g1, rw_g2, rw_kk, rw_ka, rw_rk, rw_lnx_g, rw_lnx_b,
           rw_o, na_qkv, na_q_g, na_k_g, na_rpb, na_o):
    w = dict(norm_mix_g=norm_mix_g, norm_mlp_g=norm_mlp_g, w_up=w_up, w_down=w_down,
             rw_mix=rw_mix, rw_rkv=rw_rkv, rw_w0=rw_w0, rw_w1=rw_w1, rw_w2=rw_w2,
             rw_a0=rw_a0, rw_a1=rw_a1, rw_a2=rw_a2, rw_v0=rw_v0, rw_v1=rw_v1, rw_v2=rw_v2,
             rw_g1=rw_g1, rw_g2=rw_g2, rw_kk=rw_kk, rw_ka=rw_ka, rw_rk=rw_rk,
             rw_lnx_g=rw_lnx_g, rw_lnx_b=rw_lnx_b, rw_o=rw_o,
             na_qkv=na_qkv, na_q_g=na_q_g, na_k_g=na_k_g, na_rpb=na_rpb, na_o=na_o)
    depth = norm_mix_g.shape[0]
    return _trunk(x_prompt, w, depth), _trunk(x_sample, w, depth)
```

```python
import functools

import numpy as np
import jax
import jax.numpy as jnp
from jax import lax
from jax.experimental import pallas as pl
from jax.experimental.pallas import tpu as pltpu

F32 = jnp.float32
BF16 = jnp.bfloat16

HEAD_DIM = 64
PAIR = 2 * HEAD_DIM
SEG = 256
GRID_W = 64
WIN_ROWS = 8
WIN_COLS = 16
RMS_EPS = 1e-6
GN_EPS = 64e-5
L2_EPS = 1e-24
CHUNK = 64
INV_BASE = 8
WKV_PAIRS = 8
WKV_CHUNKS = 4
NA_ROWS = 8
NA_PAIRS = 2
LOG2E = float(np.log2(np.e))
NEG_BIG = -1e30
DECAY_SCALE = float(np.exp(-0.5))
VMEM_LIMIT = 56 * 1024 * 1024


def _dot(a, b):
    return jnp.dot(a, b, preferred_element_type=F32)


def _dot_nt(a, b):
    return lax.dot_general(a, b, (((1,), (1,)), ((), ())), preferred_element_type=F32)


def _split2(x):
    hi = x.astype(BF16)
    lo = (x - hi.astype(F32)).astype(BF16)
    return hi, lo


def _split3(x):
    hi = x.astype(BF16)
    r1 = x - hi.astype(F32)
    mid = r1.astype(BF16)
    lo = (r1 - mid.astype(F32)).astype(BF16)
    return hi, mid, lo


def _segsum(x, e):
    outs = []
    for j in range(x.shape[1] // SEG):
        hi, lo = _split2(x[:, j * SEG:(j + 1) * SEG])
        outs.append(_dot(hi, e) + _dot(lo, e))
    return jnp.concatenate(outs, axis=1) if len(outs) > 1 else outs[0]


def _rms(x, g):
    return x * lax.rsqrt(jnp.mean(x * x, axis=-1, keepdims=True) + RMS_EPS) * g


def _const_spec(shape):
    nd = len(shape)
    return pl.BlockSpec(shape, lambda *_: (0,) * nd, pipeline_mode=pl.Buffered(1))


def _params(sem):
    return pltpu.CompilerParams(dimension_semantics=sem, vmem_limit_bytes=VMEM_LIMIT)


def _rwkv_pre_kernel(has_vres, seq_tiles, *refs):
    if has_vres:
        (x_ref, xp_ref, xn_ref, vf_ref, gn_ref, mix_ref, wrkv_ref, dn_ref, w2_ref, a2_ref, g2_ref,
         v2_ref, vec_ref, e_ref,
         r_o, v_o, kk_o, bv_o, g_o, lw0_o, lw1_o, kd0_o, kd1_o, b0_o, b1_o) = refs
    else:
        (x_ref, xp_ref, xn_ref, gn_ref, mix_ref, wrkv_ref, dn_ref, w2_ref, a2_ref, g2_ref,
         vec_ref, e_ref,
         r_o, v_o, kk_o, bv_o, g_o, lw0_o, lw1_o, kd0_o, kd1_o, b0_o, b1_o) = refs
    i = pl.program_id(0)
    tm, c = x_ref.shape
    gn = gn_ref[...]
    h = _rms(x_ref[...], gn)
    hp = _rms(xp_ref[7:8, :], gn)
    hn = _rms(xn_ref[0:1, :], gn)
    it = i % seq_tiles
    hp = jnp.where(it == 0, 0.0, hp)
    hn = jnp.where(it == seq_tiles - 1, 0.0, hn)
    row = lax.broadcasted_iota(jnp.int32, (tm, c), 0)
    h_prev = jnp.where(row == 0, hp, pltpu.roll(h, 1, 0))
    h_next = jnp.where(row == tm - 1, hn, pltpu.roll(h, tm - 1, 0))
    xx = 0.5 * (h_prev + h_next) - h

    def mixed(j):
        return (h + xx * mix_ref[j:j + 1, :]).astype(BF16)

    xv = mixed(2)
    r = _dot(mixed(0), wrkv_ref[0])
    k = _dot(mixed(1), wrkv_ref[1])
    v = _dot(xv, wrkv_ref[2])
    wl = jnp.tanh(_dot(mixed(3), dn_ref[:, 0:128])).astype(BF16)
    al = _dot(mixed(4), dn_ref[:, 128:256]).astype(BF16)
    gl = jax.nn.sigmoid(_dot(mixed(5), dn_ref[:, 256:512])).astype(BF16)
    g_o[...] = _dot(gl, g2_ref[...])
    if has_vres:
        vl = _dot(xv, dn_ref[:, 512:640]).astype(BF16)
        gate = jax.nn.sigmoid(vec_ref[8:9, :] + _dot(vl, v2_ref[...]))
        v = v + (vf_ref[...] - v) * gate
    v_o[...] = v
    r_o[...] = r
    e = e_ref[...]
    kk = k * vec_ref[4:5, :]
    kk = kk * lax.rsqrt(jnp.maximum(_segsum(kk * kk, e), L2_EPS))
    kk_o[...] = kk
    k_a = vec_ref[5:6, :]
    r_k = vec_ref[6:7, :]
    w_up = _dot(wl, w2_ref[...])
    a_up = _dot(al, a2_ref[...])
    bonus_arg = None
    for d, (lw_o, kd_o, b_o) in enumerate(((lw0_o, kd0_o, b0_o), (lw1_o, kd1_o, b1_o))):
        z = vec_ref[d:d + 1, :] + w_up[:, d * c:(d + 1) * c]
        lw_o[...] = -DECAY_SCALE * jax.nn.sigmoid(z)
        iclr = jax.nn.sigmoid(vec_ref[2 + d:3 + d, :] + a_up[:, d * c:(d + 1) * c])
        k_d = k * (1.0 + (iclr - 1.0) * k_a)
        kd_o[...] = k_d
        b_o[...] = kk * iclr
        term = r * k_d * r_k
        bonus_arg = term if bonus_arg is None else bonus_arg + term
    bv_o[...] = _segsum(bonus_arg, e) * v


def _rwkv_pre(x, v_first, p, seq_len, tm):
    n, c = x.shape
    has_vres = v_first is not None
    seq_tiles = seq_len // tm
    row_spec = pl.BlockSpec((tm, c), lambda i: (i, 0))
    nblk8 = n // 8
    prev_spec = pl.BlockSpec((8, c), lambda i: (jnp.maximum(i * (tm // 8) - 1, 0), 0))
    next_spec = pl.BlockSpec((8, c), lambda i: (jnp.minimum((i + 1) * (tm // 8), nblk8 - 1), 0))
    ins = [x, x, x]
    specs = [row_spec, prev_spec, next_spec]
    if has_vres:
        ins.append(v_first)
        specs.append(row_spec)
    consts = [p['gn'], p['mix'], p['wrkv'], p['dn'], p['w2'], p['a2'], p['g2']]
    if has_vres:
        consts.append(p['v2'])
    consts += [p['vec'], p['e']]
    ins += consts
    specs += [_const_spec(a.shape) for a in consts]
    out_shape = [jax.ShapeDtypeStruct((n, c), F32)] * 11
    return pl.pallas_call(
        functools.partial(_rwkv_pre_kernel, has_vres, seq_tiles),
        grid=(n // tm,),
        in_specs=specs,
        out_specs=[row_spec] * 11,
        out_shape=out_shape,
        compiler_params=_params(("parallel",)),
        name="rwkv_pre",
    )(*ins)


def _wkv_masks(reverse):
    L = CHUNK
    row = lax.broadcasted_iota(jnp.int32, (L, PAIR), 0)
    lane = lax.broadcasted_iota(jnp.int32, (L, PAIR), 1)
    col = lane & (HEAD_DIM - 1)

    def same_block(n):
        return (col // n) == (row // n)

    offs = []
    size = INV_BASE
    while size < L:
        hi_r = (row & (2 * size - 1)) >= size
        hi_c = (col & (2 * size - 1)) >= size
        offs.append(same_block(2 * size) & ((hi_c & ~hi_r) if reverse else (hi_r & ~hi_c)))
        size *= 2
    ti = lax.broadcasted_iota(jnp.int32, (L, L), 0)
    tj = lax.broadcasted_iota(jnp.int32, (L, L), 1)
    krow = lax.broadcasted_iota(jnp.int32, (PAIR, PAIR), 0)
    vcol = lax.broadcasted_iota(jnp.int32, (PAIR, PAIR), 1)
    return dict(
        head0=lane < HEAD_DIM,
        strict=(col > row) if reverse else (col < row),
        incl=(col >= row) if reverse else (col <= row),
        eye=jnp.where(col == row, 1.0, 0.0),
        base=same_block(INV_BASE),
        offs=offs,
        tri=jnp.where((tj >= ti) if reverse else (tj <= ti), 1.0, 0.0).astype(BF16),
        same_head=(krow // HEAD_DIM) == (vcol // HEAD_DIM),
        last=0 if reverse else L - 1,
    )


def _bd(q, head0):
    return jnp.concatenate([jnp.where(head0, q, 0.0), jnp.where(head0, 0.0, q)], axis=0).astype(BF16)


def _wkv_prepare(probs, m):
    L = CHUNK
    head0 = m['head0']
    lcs = []
    for (_, _, _, lw, _, _) in probs:
        hi, mid, lo = _split3(lw)
        lcs.append(_dot(m['tri'], hi) + _dot(m['tri'], mid) + _dot(m['tri'], lo))
    xs, yts, zts, dcs = [], [], [], []
    for (r, v, kk, lw, kd, b), lc in zip(probs, lcs):
        lc_end = lc[m['last']:m['last'] + 1, :]
        e_inv = jnp.exp(-lc)
        e_rem = jnp.exp(lc_end - lc)
        a_t = -(kk * jnp.exp(lc - lw))
        r_t = r * jnp.exp(lc)
        xs.append(jnp.concatenate([a_t, r_t], axis=0).astype(BF16))
        yts.append(jnp.concatenate([_bd(b * e_inv, head0), _bd(kd * e_inv, head0)], axis=0))
        z = jnp.concatenate([b * e_rem, kd * e_rem], axis=0)
        zts.append(z.T.astype(BF16))
        dcs.append(jnp.broadcast_to(jnp.exp(lc_end), (PAIR, PAIR)).T)
    scs = [_dot_nt(x, yt) for x, yt in zip(xs, yts)]
    abs_ = [jnp.where(m['strict'], sc[:L, :PAIR], 0.0) for sc in scs]
    akrk = [jnp.concatenate([jnp.where(m['strict'], sc[:L, PAIR:], 0.0),
                             jnp.where(m['incl'], sc[L:, PAIR:], 0.0)], axis=0).astype(BF16) for sc in scs]
    rbs = [jnp.where(m['incl'], sc[L:, :PAIR], 0.0).astype(BF16) for sc in scs]
    xvs = [_dot(a, _bd(p[1], head0)) for a, p in zip(akrk, probs)]

    pws = [jnp.where(m['base'], ab, 0.0) for ab in abs_]
    ts = [m['eye'] + pw for pw in pws]
    n = 2
    while n < INV_BASE:
        pws = [_dot(pw.astype(BF16), _bd(pw, head0)) for pw in pws]
        ts = [t + _dot(t.astype(BF16), _bd(pw, head0)) for t, pw in zip(ts, pws)]
        n *= 2
    for off in m['offs']:
        tas = [_dot(t.astype(BF16), _bd(jnp.where(off, ab, 0.0), head0)) for t, ab in zip(ts, abs_)]
        ts = [t + _dot(ta.astype(BF16), _bd(t, head0)) for t, ta in zip(ts, tas)]
    return [dict(x=x, t=t.astype(BF16), xv=xv, rb=rb, zt=zt, dc=dc, v=p[1])
            for x, t, xv, rb, zt, dc, p in zip(xs, ts, xvs, rbs, zts, dcs, probs)]


def _wkv_advance(preps, states, m):
    L = CHUNK
    head0 = m['head0']
    xss = [_dot(q['x'], s.astype(BF16)) for q, s in zip(preps, states)]
    us = [_dot(q['t'], _bd(xs[:L] + q['xv'][:L], head0)) for q, xs in zip(preps, xss)]
    ys = [xs[L:] + q['xv'][L:] + _dot(q['rb'], _bd(u, head0)) for q, xs, u in zip(preps, xss, us)]
    upds = [_dot(q['zt'], jnp.concatenate([u, q['v']], axis=0).astype(BF16)) for q, u in zip(preps, us)]
    new = [s * q['dc'] + jnp.where(m['same_head'], upd, 0.0) for q, s, upd in zip(preps, states, upds)]
    return ys, new


def _wkv_kernel(npairs, nchunks, *refs):
    fwd = refs[0:6]
    bwd = refs[6:12]
    y_f, y_b, s_f, s_b = refs[12:16]

    @pl.when(pl.program_id(2) == 0)
    def _():
        s_f[...] = jnp.zeros_like(s_f)
        s_b[...] = jnp.zeros_like(s_b)

    for ins, y_ref, s_ref, reverse in ((fwd, y_f, s_f, False), (bwd, y_b, s_b, True)):
        m = _wkv_masks(reverse)
        probs = []
        for ck in range(nchunks):
            for p in range(npairs):
                probs.append(tuple(ref[ck * CHUNK:(ck + 1) * CHUNK, p * PAIR:(p + 1) * PAIR] for ref in ins))
        preps = _wkv_prepare(probs, m)
        states = [s_ref[p] for p in range(npairs)]
        for ck in (reversed(range(nchunks)) if reverse else range(nchunks)):
            ys, states = _wkv_advance(preps[ck * npairs:(ck + 1) * npairs], states, m)
            for p, y in enumerate(ys):
                y_ref[ck * CHUNK:(ck + 1) * CHUNK, p * PAIR:(p + 1) * PAIR] = y
        for p, s in enumerate(states):
            s_ref[p] = s


def _wkv(r, v, kk, lw0, lw1, kd0, kd1, b0, b1, batch, seq_len, npairs, nchunks):
    n, c = r.shape
    rows = nchunks * CHUNK
    nb = seq_len // rows
    width = npairs * PAIR
    fwd_spec = pl.BlockSpec((rows, width), lambda bi, hg, ci: (bi * nb + ci, hg))
    bwd_spec = pl.BlockSpec((rows, width), lambda bi, hg, ci: (bi * nb + nb - 1 - ci, hg))
    ins = [r, v, kk, lw0, kd0, b0, r, v, kk, lw1, kd1, b1]
    return pl.pallas_call(
        functools.partial(_wkv_kernel, npairs, nchunks),
        grid=(batch, c // width, nb),
        in_specs=[fwd_spec] * 6 + [bwd_spec] * 6,
        out_specs=[fwd_spec, bwd_spec],
        out_shape=[jax.ShapeDtypeStruct((n, c), F32)] * 2,
        scratch_shapes=[pltpu.VMEM((npairs, PAIR, PAIR), F32)] * 2,
        compiler_params=_params(("parallel", "parallel", "arbitrary")),
        name="wkv_scan",
    )(*ins)


def _mlp_tail(x1, gm_ref, wup_ref, wdn_ref, out_ref):
    hb = _rms(x1, gm_ref[...]).astype(BF16)
    c = x1.shape[1]
    acc = x1
    for j in range(wup_ref.shape[1] // c):
        u = jnp.maximum(_dot(hb, wup_ref[:, j * c:(j + 1) * c]), 0.0)
        acc = acc + _dot((u * u).astype(BF16), wdn_ref[j * c:(j + 1) * c, :])
    out_ref[...] = acc


def _na_post_kernel(x_ref, o_ref, wo_ref, gm_ref, wup_ref, wdn_ref, out_ref):
    x1 = x_ref[...] + _dot(o_ref[...], wo_ref[...])
    _mlp_tail(x1, gm_ref, wup_ref, wdn_ref, out_ref)


def _rwkv_post_kernel(x_ref, yf_ref, yb_ref, bv_ref, g_ref, ln_ref, e_ref, wo_ref, gm_ref, wup_ref,
                      wdn_ref, out_ref):
    e = e_ref[...]
    y = yf_ref[...] + yb_ref[...]
    inv_n = 1.0 / HEAD_DIM
    d = y - _segsum(y, e) * inv_n
    var = _segsum(d * d, e) * inv_n
    yn = d * lax.rsqrt(var + GN_EPS) * ln_ref[0:1, :] + ln_ref[1:2, :] + bv_ref[...]
    x1 = x_ref[...] + _dot((yn * g_ref[...]).astype(BF16), wo_ref[...])
    _mlp_tail(x1, gm_ref, wup_ref, wdn_ref, out_ref)


def _post(kernel, x, acts, consts, tm):
    n, c = x.shape
    row_spec = pl.BlockSpec((tm, c), lambda i: (i, 0))
    ins = [x] + list(acts) + list(consts)
    specs = [row_spec] * (1 + len(acts)) + [_const_spec(a.shape) for a in consts]
    return pl.pallas_call(
        kernel,
        grid=(n // tm,),
        in_specs=specs,
        out_specs=row_spec,
        out_shape=jax.ShapeDtypeStruct((n, c), F32),
        compiler_params=_params(("parallel",)),
        name=kernel.__name__.strip("_"),
    )(*ins)


def _na_qkv_kernel(x_ref, gn_ref, w_ref, qg_ref, kg_ref, e_ref, q_o, k_o, v_o):
    hb = _rms(x_ref[...], gn_ref[...]).astype(BF16)
    c = x_ref.shape[1]
    e = e_ref[...]
    inv_n = 1.0 / HEAD_DIM
    for part, (out, gain) in enumerate(((q_o, qg_ref), (k_o, kg_ref), (v_o, None))):
        for j in range(c // SEG):
            lo = part * c + j * SEG
            acc = _dot(hb, w_ref[:, lo:lo + SEG])
            if gain is not None:
                ms = _segsum(acc * acc, e) * inv_n
                acc = acc * lax.rsqrt(ms + RMS_EPS) * gain[...]
            out[:, j * SEG:(j + 1) * SEG] = acc.astype(out.dtype)


def _na_qkv(x, p, tm):
    n, c = x.shape
    row_spec = pl.BlockSpec((tm, c), lambda i: (i, 0))
    consts = [p['gn'], p['wqkv'], p['qg'], p['kg'], p['e']]
    return pl.pallas_call(
        _na_qkv_kernel,
        grid=(n // tm,),
        in_specs=[row_spec] + [_const_spec(a.shape) for a in consts],
        out_specs=[row_spec] * 3,
        out_shape=[jax.ShapeDtypeStruct((n, c), BF16)] * 3,
        compiler_params=_params(("parallel",)),
        name="na_qkv",
    )(x, *consts)


def _na_kernel(rows, q_ref, kp_ref, kc_ref, kn_ref, vp_ref, vc_ref, vn_ref, bias_ref, o_ref, kw_ref, vw_ref):
    ib = pl.program_id(2)
    blk = NA_ROWS * GRID_W
    for j, (kr, vr) in enumerate(((kp_ref, vp_ref), (kc_ref, vc_ref), (kn_ref, vn_ref))):
        kw_ref[j * blk:(j + 1) * blk, :] = kr[...]
        vw_ref[j * blk:(j + 1) * blk, :] = vr[...]
    lane = lax.broadcasted_iota(jnp.int32, (GRID_W, PAIR), 1)
    head0 = lane < HEAD_DIM
    nkeys = WIN_ROWS * GRID_W
    probs = []
    for p in range(NA_PAIRS):
        lanes = slice(p * PAIR, (p + 1) * PAIR)
        for ii in range(NA_ROWS):
            i = ib * NA_ROWS + ii
            r0 = jnp.clip(i - WIN_ROWS // 2, 0, rows - WIN_ROWS)
            start = pl.multiple_of((r0 - ib * NA_ROWS + NA_ROWS) * GRID_W, GRID_W)
            probs.append((p, ii, lanes, pl.ds(start, nkeys), r0 - i + WIN_ROWS - 1))
    qss = []
    for p, ii, lanes, _, _ in probs:
        q = q_ref[ii * GRID_W:(ii + 1) * GRID_W, lanes]
        zero = jnp.zeros_like(q)
        qss.append(jnp.concatenate([jnp.where(head0, q, zero), jnp.where(head0, zero, q)], axis=0))
    ss = [_dot_nt(qs, kw_ref[win, lanes]) + bias_ref[p, d0] for qs, (p, _, lanes, win, d0) in zip(qss, probs)]
    prs = [jnp.exp2(s - jnp.max(s, axis=-1, keepdims=True)) for s in ss]
    inv_ls = [1.0 / jnp.sum(pr, axis=-1, keepdims=True) for pr in prs]
    pvs = [_dot(pr.astype(BF16), vw_ref[win, lanes]) * inv_l
           for pr, inv_l, (_, _, lanes, win, _) in zip(prs, inv_ls, probs)]
    for pv, (_, ii, lanes, _, _) in zip(pvs, probs):
        o = jnp.where(head0, pv[:GRID_W], pv[GRID_W:])
        o_ref[ii * GRID_W:(ii + 1) * GRID_W, lanes] = o.astype(o_ref.dtype)


def _na(q, k, v, bias, batch, seq_len):
    n, c = q.shape
    rows = seq_len // GRID_W
    nrb = rows // NA_ROWS
    blk = NA_ROWS * GRID_W
    width = NA_PAIRS * PAIR

    def spec(off):
        return pl.BlockSpec((blk, width), lambda p, bi, ib: (bi * nrb + jnp.clip(ib + off, 0, nrb - 1), p))

    bias_spec = pl.BlockSpec((NA_PAIRS,) + bias.shape[1:], lambda p, bi, ib: (p, 0, 0, 0))
    return pl.pallas_call(
        functools.partial(_na_kernel, rows),
        grid=(c // width, batch, nrb),
        in_specs=[spec(0), spec(-1), spec(0), spec(1), spec(-1), spec(0), spec(1), bias_spec],
        out_specs=spec(0),
        out_shape=jax.ShapeDtypeStruct((n, c), BF16),
        scratch_shapes=[pltpu.VMEM((3 * blk, width), BF16)] * 2,
        compiler_params=_params(("parallel", "parallel", "arbitrary")),
        name="na_attn",
    )(q, k, k, k, v, v, v, bias)


def _na_bias(rpb):
    nh = rpb.shape[0]
    qc = np.arange(GRID_W)[:, None]
    kc = np.arange(GRID_W)[None, :]
    win_start = np.clip(qc - WIN_COLS // 2, 0, GRID_W - WIN_COLS)
    valid = (kc >= win_start) & (kc < win_start + WIN_COLS)
    dc = np.clip(kc - qc + WIN_COLS - 1, 0, 2 * WIN_COLS - 2)
    tab = jnp.where(valid[None, None], rpb[:, :, dc] * LOG2E, NEG_BIG)
    dr = np.arange(WIN_ROWS)[:, None] + np.arange(WIN_ROWS)[None, :]
    win = tab[:, dr]
    win = jnp.transpose(win, (0, 1, 3, 2, 4)).reshape(nh, WIN_ROWS, GRID_W, WIN_ROWS * GRID_W)
    win = win.reshape(nh // 2, 2, WIN_ROWS, GRID_W, WIN_ROWS * GRID_W)
    return jnp.transpose(win, (0, 2, 1, 3, 4)).reshape(nh // 2, WIN_ROWS, PAIR, WIN_ROWS * GRID_W)


def _block_diag2(w):
    k, c = w.shape[1:]
    z = jnp.zeros((k, c), w.dtype)
    return jnp.concatenate([jnp.concatenate([w[0], z], 1), jnp.concatenate([z, w[1]], 1)], 0)


def _pad_to(w, axis, size):
    pad = [(0, 0)] * w.ndim
    pad[axis] = (0, size - w.shape[axis])
    return jnp.pad(w, pad)


def _seg_matrix():
    i = np.arange(SEG) // HEAD_DIM
    return jnp.asarray(i[:, None] == i[None, :], BF16)


def _pack_rwkv(j, w):
    c = w['rw_rkv'].shape[-1]
    p = {}
    p['mix'] = _pad_to(w['rw_mix'][j], 0, 8)
    p['wrkv'] = w['rw_rkv'][j].astype(BF16)
    dn = [w['rw_w1'][j, 0], w['rw_w1'][j, 1], w['rw_a1'][j, 0], w['rw_a1'][j, 1],
          _pad_to(w['rw_g1'][j], 1, 256)]
    vec = [w['rw_w0'][j, 0], w['rw_w0'][j, 1], w['rw_a0'][j, 0], w['rw_a0'][j, 1],
           w['rw_kk'][j], w['rw_ka'][j], w['rw_rk'][j].reshape(c), jnp.zeros((c,), F32)]
    if j > 0:
        dn.append(_pad_to(w['rw_v1'][j - 1], 1, 128))
        vec.append(w['rw_v0'][j - 1])
        p['v2'] = _pad_to(w['rw_v2'][j - 1], 0, 128).astype(BF16)
    p['dn'] = jnp.concatenate(dn, axis=1).astype(BF16)
    p['vec'] = _pad_to(jnp.stack(vec), 0, 16)
    p['w2'] = _block_diag2(w['rw_w2'][j]).astype(BF16)
    p['a2'] = _block_diag2(w['rw_a2'][j]).astype(BF16)
    p['g2'] = _pad_to(w['rw_g2'][j], 0, 256).astype(BF16)
    p['ln'] = _pad_to(jnp.stack([w['rw_lnx_g'][j], w['rw_lnx_b'][j]]), 0, 8)
    p['wo'] = w['rw_o'][j].astype(BF16)
    return p


def _pack_na(j, w):
    scale = HEAD_DIM ** -0.5 * LOG2E
    return {
        'wqkv': w['na_qkv'][j].astype(BF16),
        'qg': jnp.tile(w['na_q_g'][j] * scale, SEG // HEAD_DIM)[None, :],
        'kg': jnp.tile(w['na_k_g'][j], SEG // HEAD_DIM)[None, :],
        'bias': _na_bias(w['na_rpb'][j]),
        'wo': w['na_o'][j].astype(BF16),
    }


def _trunk(x3, w, depth):
    batch, seq_len, c = x3.shape
    x = x3.reshape(batch * seq_len, c)
    e = _seg_matrix()
    v_first = None
    for layer in range(depth):
        j = layer // 2
        gn = w['norm_mix_g'][layer][None, :]
        mlp = [w['norm_mlp_g'][layer][None, :], w['w_up'][layer].astype(BF16), w['w_down'][layer].astype(BF16)]
        if layer % 2 == 0:
            p = _pack_rwkv(j, w)
            p['gn'], p['e'] = gn, e
            r, v, kk, bv, g, lw0, lw1, kd0, kd1, b0, b1 = _rwkv_pre(x, v_first, p, seq_len, 256)
            if v_first is None:
                v_first = v
            y_f, y_b = _wkv(r, v, kk, lw0, lw1, kd0, kd1, b0, b1, batch, seq_len, WKV_PAIRS, WKV_CHUNKS)
            x = _post(_rwkv_post_kernel, x, [y_f, y_b, bv, g], [p['ln'], e, p['wo']] + mlp, 256)
        else:
            p = _pack_na(j, w)
            p['gn'], p['e'] = gn, e
            q, k, v = _na_qkv(x, p, 512)
            o = _na(q, k, v, p['bias'], batch, seq_len)
            x = _post(_na_post_kernel, x, [o], [p['wo']] + mlp, 512)
    return x.reshape(batch, seq_len, c)


def kernel(x_prompt, x_sample, norm_mix_g, norm_mlp_g, w_up, w_down, rw_mix, rw_rkv, rw_w0, rw_w1, rw_w2,
           rw_a0, rw_a1, rw_a2, rw_v0, rw_v1, rw_v2, rw_g1, rw_g2, rw_kk, rw_ka, rw_rk, rw_lnx_g, rw_lnx_b,
           rw_o, na_qkv, na_q_g, na_k_g, na_rpb, na_o):
    w = dict(norm_mix_g=norm_mix_g, norm_mlp_g=norm_mlp_g, w_up=w_up, w_down=w_down,
             rw_mix=rw_mix, rw_rkv=rw_rkv, rw_w0=rw_w0, rw_w1=rw_w1, rw_w2=rw_w2,
             rw_a0=rw_a0, rw_a1=rw_a1, rw_a2=rw_a2, rw_v0=rw_v0, rw_v1=rw_v1, rw_v2=rw_v2,
             rw_g1=rw_g1, rw_g2=rw_g2, rw_kk=rw_kk, rw_ka=rw_ka, rw_rk=rw_rk,
             rw_lnx_g=rw_lnx_g, rw_lnx_b=rw_lnx_b, rw_o=rw_o,
             na_qkv=na_qkv, na_q_g=na_q_g, na_k_g=na_k_g, na_rpb=na_rpb, na_o=na_o)
    depth = norm_mix_g.shape[0]
    return _trunk(x_prompt, w, depth), _trunk(x_sample, w, depth)
```

```python
import functools

import numpy as np
import jax
import jax.numpy as jnp
from jax import lax
from jax.experimental import pallas as pl
from jax.experimental.pallas import tpu as pltpu

F32 = jnp.float32
BF16 = jnp.bfloat16

HEAD_DIM = 64
PAIR = 2 * HEAD_DIM
SEG = 256
GRID_W = 64
WIN_ROWS = 8
WIN_COLS = 16
RMS_EPS = 1e-6
GN_EPS = 64e-5
L2_EPS = 1e-24
CHUNK = 64
INV_BASE = 8
WKV_PAIRS = 8
WKV_CHUNKS = 4
NA_ROWS = 8
NA_PAIRS = 2
LOG2E = float(np.log2(np.e))
NEG_BIG = -1e30
DECAY_SCALE = float(np.exp(-0.5))
VMEM_LIMIT = 56 * 1024 * 1024


def _dot(a, b):
    return jnp.dot(a, b, preferred_element_type=F32)


def _dot_nt(a, b):
    return lax.dot_general(a, b, (((1,), (1,)), ((), ())), preferred_element_type=F32)


def _split2(x):
    hi = x.astype(BF16)
    lo = (x - hi.astype(F32)).astype(BF16)
    return hi, lo


def _split3(x):
    hi = x.astype(BF16)
    r1 = x - hi.astype(F32)
    mid = r1.astype(BF16)
    lo = (r1 - mid.astype(F32)).astype(BF16)
    return hi, mid, lo


def _segsum(x, e):
    outs = []
    for j in range(x.shape[1] // SEG):
        hi, lo = _split2(x[:, j * SEG:(j + 1) * SEG])
        outs.append(_dot(hi, e) + _dot(lo, e))
    return jnp.concatenate(outs, axis=1) if len(outs) > 1 else outs[0]


def _sigmoid(z):
    return 0.5 * jnp.tanh(0.5 * z) + 0.5


def _rms(x, g):
    return x * lax.rsqrt(jnp.mean(x * x, axis=-1, keepdims=True) + RMS_EPS) * g


def _const_spec(shape):
    nd = len(shape)
    return pl.BlockSpec(shape, lambda *_: (0,) * nd, pipeline_mode=pl.Buffered(1))


def _params(sem):
    return pltpu.CompilerParams(dimension_semantics=sem, vmem_limit_bytes=VMEM_LIMIT)


def _rwkv_pre_kernel(has_vres, seq_tiles, *refs):
    if has_vres:
        (x_ref, xp_ref, xn_ref, vf_ref, gn_ref, mix_ref, wrkv_ref, dn_ref, w2_ref, a2_ref, g2_ref,
         v2_ref, vec_ref, e_ref,
         r_o, v_o, kk_o, bv_o, g_o, lw0_o, lw1_o, kd0_o, kd1_o, b0_o, b1_o) = refs
    else:
        (x_ref, xp_ref, xn_ref, gn_ref, mix_ref, wrkv_ref, dn_ref, w2_ref, a2_ref, g2_ref,
         vec_ref, e_ref,
         r_o, v_o, kk_o, bv_o, g_o, lw0_o, lw1_o, kd0_o, kd1_o, b0_o, b1_o) = refs
    i = pl.program_id(0)
    tm, c = x_ref.shape
    gn = gn_ref[...]
    h = _rms(x_ref[...], gn)
    hp = _rms(xp_ref[7:8, :], gn)
    hn = _rms(xn_ref[0:1, :], gn)
    it = i % seq_tiles
    hp = jnp.where(it == 0, 0.0, hp)
    hn = jnp.where(it == seq_tiles - 1, 0.0, hn)
    row = lax.broadcasted_iota(jnp.int32, (tm, c), 0)
    h_prev = jnp.where(row == 0, hp, pltpu.roll(h, 1, 0))
    h_next = jnp.where(row == tm - 1, hn, pltpu.roll(h, tm - 1, 0))
    xx = 0.5 * (h_prev + h_next) - h

    def mixed(j):
        return (h + xx * mix_ref[j:j + 1, :]).astype(BF16)

    xr, xk, xv = mixed(0), mixed(1), mixed(2)
    wl = jnp.tanh(_dot(mixed(3), dn_ref[:, 0:128])).astype(BF16)
    al = _dot(mixed(4), dn_ref[:, 128:256]).astype(BF16)
    gl = _sigmoid(_dot(mixed(5), dn_ref[:, 256:512])).astype(BF16)
    vl = _dot(xv, dn_ref[:, 512:640]).astype(BF16) if has_vres else None
    e = e_ref[...]

    def project(cols):
        out = dict(
            r=_dot(xr, wrkv_ref[0, :, cols]), k=_dot(xk, wrkv_ref[1, :, cols]), v=_dot(xv, wrkv_ref[2, :, cols]),
            g=_dot(gl, g2_ref[:, cols]),
            w=[_dot(wl, w2_ref[:, d * c + cols.start:d * c + cols.stop]) for d in range(2)],
            a=[_dot(al, a2_ref[:, d * c + cols.start:d * c + cols.stop]) for d in range(2)])
        if has_vres:
            out['vg'] = _dot(vl, v2_ref[:, cols])
        return out

    def finish(cols, q):
        r, k, v = q['r'], q['k'], q['v']
        g_o[:, cols] = q['g']
        if has_vres:
            v = v + (vf_ref[:, cols] - v) * _sigmoid(vec_ref[8:9, cols] + q['vg'])
        v_o[:, cols] = v
        r_o[:, cols] = r
        kk = k * vec_ref[4:5, cols]
        kk = kk * lax.rsqrt(jnp.maximum(_segsum(kk * kk, e), L2_EPS))
        kk_o[:, cols] = kk
        k_a = vec_ref[5:6, cols]
        r_k = vec_ref[6:7, cols]
        bonus_arg = None
        for d, (lw_o, kd_o, b_o) in enumerate(((lw0_o, kd0_o, b0_o), (lw1_o, kd1_o, b1_o))):
            z = vec_ref[d:d + 1, cols] + q['w'][d]
            lw_o[:, cols] = -DECAY_SCALE * _sigmoid(z)
            iclr = _sigmoid(vec_ref[2 + d:3 + d, cols] + q['a'][d])
            k_d = k * (1.0 + (iclr - 1.0) * k_a)
            kd_o[:, cols] = k_d
            b_o[:, cols] = kk * iclr
            term = r * k_d * r_k
            bonus_arg = term if bonus_arg is None else bonus_arg + term
        bv_o[:, cols] = _segsum(bonus_arg, e) * v

    pending = None
    for j in range(c // SEG):
        cols = slice(j * SEG, (j + 1) * SEG)
        q = project(cols)
        if pending is not None:
            finish(*pending)
        pending = (cols, q)
    finish(*pending)


def _rwkv_pre(x, v_first, p, seq_len, tm):
    n, c = x.shape
    has_vres = v_first is not None
    seq_tiles = seq_len // tm
    row_spec = pl.BlockSpec((tm, c), lambda i: (i, 0))
    nblk8 = n // 8
    prev_spec = pl.BlockSpec((8, c), lambda i: (jnp.maximum(i * (tm // 8) - 1, 0), 0))
    next_spec = pl.BlockSpec((8, c), lambda i: (jnp.minimum((i + 1) * (tm // 8), nblk8 - 1), 0))
    ins = [x, x, x]
    specs = [row_spec, prev_spec, next_spec]
    if has_vres:
        ins.append(v_first)
        specs.append(row_spec)
    consts = [p['gn'], p['mix'], p['wrkv'], p['dn'], p['w2'], p['a2'], p['g2']]
    if has_vres:
        consts.append(p['v2'])
    consts += [p['vec'], p['e']]
    ins += consts
    specs += [_const_spec(a.shape) for a in consts]
    out_shape = [jax.ShapeDtypeStruct((n, c), F32)] * 11
    return pl.pallas_call(
        functools.partial(_rwkv_pre_kernel, has_vres, seq_tiles),
        grid=(n // tm,),
        in_specs=specs,
        out_specs=[row_spec] * 11,
        out_shape=out_shape,
        compiler_params=_params(("parallel",)),
        name="rwkv_pre",
    )(*ins)


def _wkv_masks(reverse):
    L = CHUNK
    row = lax.broadcasted_iota(jnp.int32, (L, PAIR), 0)
    lane = lax.broadcasted_iota(jnp.int32, (L, PAIR), 1)
    col = lane & (HEAD_DIM - 1)

    def same_block(n):
        return (col // n) == (row // n)

    offs = []
    size = INV_BASE
    while size < L:
        hi_r = (row & (2 * size - 1)) >= size
        hi_c = (col & (2 * size - 1)) >= size
        offs.append(same_block(2 * size) & ((hi_c & ~hi_r) if reverse else (hi_r & ~hi_c)))
        size *= 2
    ti = lax.broadcasted_iota(jnp.int32, (L, L), 0)
    tj = lax.broadcasted_iota(jnp.int32, (L, L), 1)
    krow = lax.broadcasted_iota(jnp.int32, (PAIR, PAIR), 0)
    vcol = lax.broadcasted_iota(jnp.int32, (PAIR, PAIR), 1)
    return dict(
        head0=lane < HEAD_DIM,
        strict=(col > row) if reverse else (col < row),
        incl=(col >= row) if reverse else (col <= row),
        eye=jnp.where(col == row, 1.0, 0.0),
        base=same_block(INV_BASE),
        offs=offs,
        tri=jnp.where((tj >= ti) if reverse else (tj <= ti), 1.0, 0.0).astype(BF16),
        same_head=(krow // HEAD_DIM) == (vcol // HEAD_DIM),
        last=0 if reverse else L - 1,
    )


def _bd(q, head0):
    return jnp.concatenate([jnp.where(head0, q, 0.0), jnp.where(head0, 0.0, q)], axis=0).astype(BF16)


def _wkv_prepare(probs, m):
    L = CHUNK
    head0 = m['head0']
    lcs = []
    for (_, _, _, lw, _, _) in probs:
        hi, lo = _split2(lw)
        lcs.append(_dot(m['tri'], hi) + _dot(m['tri'], lo))
    xs, yts, zts, dcs = [], [], [], []
    for (r, v, kk, lw, kd, b), lc in zip(probs, lcs):
        lc_end = lc[m['last']:m['last'] + 1, :]
        e_inv = jnp.exp(-lc)
        e_rem = jnp.exp(lc_end - lc)
        a_t = -(kk * jnp.exp(lc - lw))
        r_t = r * jnp.exp(lc)
        xs.append(jnp.concatenate([a_t, r_t], axis=0).astype(BF16))
        yts.append(jnp.concatenate([_bd(b * e_inv, head0), _bd(kd * e_inv, head0)], axis=0))
        z = jnp.concatenate([b * e_rem, kd * e_rem], axis=0)
        zts.append(z.T.astype(BF16))
        dcs.append(jnp.broadcast_to(jnp.exp(lc_end), (PAIR, PAIR)).T)
    scs = [_dot_nt(x, yt) for x, yt in zip(xs, yts)]
    abs_ = [jnp.where(m['strict'], sc[:L, :PAIR], 0.0) for sc in scs]
    akrk = [jnp.concatenate([jnp.where(m['strict'], sc[:L, PAIR:], 0.0),
                             jnp.where(m['incl'], sc[L:, PAIR:], 0.0)], axis=0).astype(BF16) for sc in scs]
    rbs = [jnp.where(m['incl'], sc[L:, :PAIR], 0.0).astype(BF16) for sc in scs]
    xvs = [_dot(a, _bd(p[1], head0)) for a, p in zip(akrk, probs)]

    pws = [jnp.where(m['base'], ab, 0.0) for ab in abs_]
    ts = [m['eye'] + pw for pw in pws]
    n = 2
    while n < INV_BASE:
        pws = [_dot(pw.astype(BF16), _bd(pw, head0)) for pw in pws]
        ts = [t + _dot(t.astype(BF16), _bd(pw, head0)) for t, pw in zip(ts, pws)]
        n *= 2
    for off in m['offs']:
        tas = [_dot(t.astype(BF16), _bd(jnp.where(off, ab, 0.0), head0)) for t, ab in zip(ts, abs_)]
        ts = [t + _dot(ta.astype(BF16), _bd(t, head0)) for t, ta in zip(ts, tas)]
    return [dict(x=x, t=t.astype(BF16), xv=xv, rb=rb, zt=zt, dc=dc, v=p[1])
            for x, t, xv, rb, zt, dc, p in zip(xs, ts, xvs, rbs, zts, dcs, probs)]


def _wkv_advance(preps, states, m):
    L = CHUNK
    head0 = m['head0']
    xss = [_dot(q['x'], s.astype(BF16)) for q, s in zip(preps, states)]
    us = [_dot(q['t'], _bd(xs[:L] + q['xv'][:L], head0)) for q, xs in zip(preps, xss)]
    ys = [xs[L:] + q['xv'][L:] + _dot(q['rb'], _bd(u, head0)) for q, xs, u in zip(preps, xss, us)]
    upds = [_dot(q['zt'], jnp.concatenate([u, q['v']], axis=0).astype(BF16)) for q, u in zip(preps, us)]
    new = [s * q['dc'] + jnp.where(m['same_head'], upd, 0.0) for q, s, upd in zip(preps, states, upds)]
    return ys, new


def _wkv_kernel(npairs, nchunks, *refs):
    fwd = refs[0:6]
    bwd = refs[6:12]
    y_f, y_b, s_f, s_b = refs[12:16]

    @pl.when(pl.program_id(2) == 0)
    def _():
        s_f[...] = jnp.zeros_like(s_f)
        s_b[...] = jnp.zeros_like(s_b)

    for ins, y_ref, s_ref, reverse in ((fwd, y_f, s_f, False), (bwd, y_b, s_b, True)):
        m = _wkv_masks(reverse)
        probs = []
        for ck in range(nchunks):
            for p in range(npairs):
                probs.append(tuple(ref[ck * CHUNK:(ck + 1) * CHUNK, p * PAIR:(p + 1) * PAIR] for ref in ins))
        preps = _wkv_prepare(probs, m)
        states = [s_ref[p] for p in range(npairs)]
        for ck in (reversed(range(nchunks)) if reverse else range(nchunks)):
            ys, states = _wkv_advance(preps[ck * npairs:(ck + 1) * npairs], states, m)
            for p, y in enumerate(ys):
                y_ref[ck * CHUNK:(ck + 1) * CHUNK, p * PAIR:(p + 1) * PAIR] = y
        for p, s in enumerate(states):
            s_ref[p] = s


def _wkv(r, v, kk, lw0, lw1, kd0, kd1, b0, b1, batch, seq_len, npairs, nchunks):
    n, c = r.shape
    rows = nchunks * CHUNK
    nb = seq_len // rows
    width = npairs * PAIR
    fwd_spec = pl.BlockSpec((rows, width), lambda bi, hg, ci: (bi * nb + ci, hg))
    bwd_spec = pl.BlockSpec((rows, width), lambda bi, hg, ci: (bi * nb + nb - 1 - ci, hg))
    ins = [r, v, kk, lw0, kd0, b0, r, v, kk, lw1, kd1, b1]
    return pl.pallas_call(
        functools.partial(_wkv_kernel, npairs, nchunks),
        grid=(batch, c // width, nb),
        in_specs=[fwd_spec] * 6 + [bwd_spec] * 6,
        out_specs=[fwd_spec, bwd_spec],
        out_shape=[jax.ShapeDtypeStruct((n, c), F32)] * 2,
        scratch_shapes=[pltpu.VMEM((npairs, PAIR, PAIR), F32)] * 2,
        compiler_params=_params(("parallel", "parallel", "arbitrary")),
        name="wkv_scan",
    )(*ins)


def _mlp_tail(x1, gm_ref, wup_ref, wdn_ref, out_ref):
    hb = _rms(x1, gm_ref[...]).astype(BF16)
    c = x1.shape[1]
    acc = x1
    nblk = wup_ref.shape[1] // c
    u = _dot(hb, wup_ref[:, 0:c])
    for j in range(nblk):
        u_next = _dot(hb, wup_ref[:, (j + 1) * c:(j + 2) * c]) if j + 1 < nblk else None
        u = jnp.maximum(u, 0.0)
        acc = acc + _dot((u * u).astype(BF16), wdn_ref[j * c:(j + 1) * c, :])
        u = u_next
    out_ref[...] = acc


def _na_post_kernel(x_ref, o_ref, wo_ref, gm_ref, wup_ref, wdn_ref, out_ref):
    x1 = x_ref[...] + _dot(o_ref[...], wo_ref[...])
    _mlp_tail(x1, gm_ref, wup_ref, wdn_ref, out_ref)


def _rwkv_post_kernel(x_ref, yf_ref, yb_ref, bv_ref, g_ref, ln_ref, e_ref, wo_ref, gm_ref, wup_ref,
                      wdn_ref, out_ref):
    e = e_ref[...]
    y = yf_ref[...] + yb_ref[...]
    inv_n = 1.0 / HEAD_DIM
    d = y - _segsum(y, e) * inv_n
    var = _segsum(d * d, e) * inv_n
    yn = d * lax.rsqrt(var + GN_EPS) * ln_ref[0:1, :] + ln_ref[1:2, :] + bv_ref[...]
    x1 = x_ref[...] + _dot((yn * g_ref[...]).astype(BF16), wo_ref[...])
    _mlp_tail(x1, gm_ref, wup_ref, wdn_ref, out_ref)


def _post(kernel, x, acts, consts, tm):
    n, c = x.shape
    row_spec = pl.BlockSpec((tm, c), lambda i: (i, 0))
    ins = [x] + list(acts) + list(consts)
    specs = [row_spec] * (1 + len(acts)) + [_const_spec(a.shape) for a in consts]
    return pl.pallas_call(
        kernel,
        grid=(n // tm,),
        in_specs=specs,
        out_specs=row_spec,
        out_shape=jax.ShapeDtypeStruct((n, c), F32),
        compiler_params=_params(("parallel",)),
        name=kernel.__name__.strip("_"),
    )(*ins)


def _na_qkv_kernel(x_ref, gn_ref, w_ref, qg_ref, kg_ref, e_ref, q_o, k_o, v_o):
    hb = _rms(x_ref[...], gn_ref[...]).astype(BF16)
    c = x_ref.shape[1]
    e = e_ref[...]
    inv_n = 1.0 / HEAD_DIM
    blocks = [(out, gain, part * c + j * SEG, j * SEG)
              for part, (out, gain) in enumerate(((q_o, qg_ref), (k_o, kg_ref), (v_o, None)))
              for j in range(c // SEG)]

    def finish(acc, out, gain, col):
        if gain is not None:
            ms = _segsum(acc * acc, e) * inv_n
            acc = acc * lax.rsqrt(ms + RMS_EPS) * gain[...]
        out[:, col:col + SEG] = acc.astype(out.dtype)

    pending = None
    for out, gain, wcol, col in blocks:
        acc = _dot(hb, w_ref[:, wcol:wcol + SEG])
        if pending is not None:
            finish(*pending)
        pending = (acc, out, gain, col)
    finish(*pending)


def _na_qkv(x, p, tm):
    n, c = x.shape
    row_spec = pl.BlockSpec((tm, c), lambda i: (i, 0))
    consts = [p['gn'], p['wqkv'], p['qg'], p['kg'], p['e']]
    return pl.pallas_call(
        _na_qkv_kernel,
        grid=(n // tm,),
        in_specs=[row_spec] + [_const_spec(a.shape) for a in consts],
        out_specs=[row_spec] * 3,
        out_shape=[jax.ShapeDtypeStruct((n, c), BF16)] * 3,
        compiler_params=_params(("parallel",)),
        name="na_qkv",
    )(x, *consts)


def _na_kernel(rows, q_ref, kp_ref, kc_ref, kn_ref, vp_ref, vc_ref, vn_ref, bias_ref, o_ref, kw_ref, vw_ref):
    ib = pl.program_id(2)
    blk = NA_ROWS * GRID_W
    for j, (kr, vr) in enumerate(((kp_ref, vp_ref), (kc_ref, vc_ref), (kn_ref, vn_ref))):
        kw_ref[j * blk:(j + 1) * blk, :] = kr[...]
        vw_ref[j * blk:(j + 1) * blk, :] = vr[...]
    lane = lax.broadcasted_iota(jnp.int32, (GRID_W, PAIR), 1)
    head0 = lane < HEAD_DIM
    nkeys = WIN_ROWS * GRID_W
    probs = []
    for p in range(NA_PAIRS):
        lanes = slice(p * PAIR, (p + 1) * PAIR)
        for ii in range(NA_ROWS):
            i = ib * NA_ROWS + ii
            r0 = jnp.clip(i - WIN_ROWS // 2, 0, rows - WIN_ROWS)
            start = pl.multiple_of((r0 - ib * NA_ROWS + NA_ROWS) * GRID_W, GRID_W)
            probs.append((p, ii, lanes, pl.ds(start, nkeys), r0 - i + WIN_ROWS - 1))
    qss = []
    for p, ii, lanes, _, _ in probs:
        q = q_ref[ii * GRID_W:(ii + 1) * GRID_W, lanes]
        zero = jnp.zeros_like(q)
        qss.append(jnp.concatenate([jnp.where(head0, q, zero), jnp.where(head0, zero, q)], axis=0))
    ss = [_dot_nt(qs, kw_ref[win, lanes]) + bias_ref[p, d0] for qs, (p, _, lanes, win, d0) in zip(qss, probs)]
    prs = [jnp.exp2(s - jnp.max(s, axis=-1, keepdims=True)) for s in ss]
    inv_ls = [1.0 / jnp.sum(pr, axis=-1, keepdims=True) for pr in prs]
    pvs = [_dot(pr.astype(BF16), vw_ref[win, lanes]) * inv_l
           for pr, inv_l, (_, _, lanes, win, _) in zip(prs, inv_ls, probs)]
    for pv, (_, ii, lanes, _, _) in zip(pvs, probs):
        o = jnp.where(head0, pv[:GRID_W], pv[GRID_W:])
        o_ref[ii * GRID_W:(ii + 1) * GRID_W, lanes] = o.astype(o_ref.dtype)


def _na(q, k, v, bias, batch, seq_len):
    n, c = q.shape
    rows = seq_len // GRID_W
    nrb = rows // NA_ROWS
    blk = NA_ROWS * GRID_W
    width = NA_PAIRS * PAIR

    def spec(off):
        return pl.BlockSpec((blk, width), lambda p, bi, ib: (bi * nrb + jnp.clip(ib + off, 0, nrb - 1), p))

    bias_spec = pl.BlockSpec((NA_PAIRS,) + bias.shape[1:], lambda p, bi, ib: (p, 0, 0, 0))
    return pl.pallas_call(
        functools.partial(_na_kernel, rows),
        grid=(c // width, batch, nrb),
        in_specs=[spec(0), spec(-1), spec(0), spec(1), spec(-1), spec(0), spec(1), bias_spec],
        out_specs=spec(0),
        out_shape=jax.ShapeDtypeStruct((n, c), BF16),
        scratch_shapes=[pltpu.VMEM((3 * blk, width), BF16)] * 2,
        compiler_params=_params(("parallel", "parallel", "arbitrary")),
        name="na_attn",
    )(q, k, k, k, v, v, v, bias)


def _na_bias(rpb):
    nh = rpb.shape[0]
    qc = np.arange(GRID_W)[:, None]
    kc = np.arange(GRID_W)[None, :]
    win_start = np.clip(qc - WIN_COLS // 2, 0, GRID_W - WIN_COLS)
    valid = (kc >= win_start) & (kc < win_start + WIN_COLS)
    dc = np.clip(kc - qc + WIN_COLS - 1, 0, 2 * WIN_COLS - 2)
    tab = jnp.where(valid[None, None], rpb[:, :, dc] * LOG2E, NEG_BIG)
    dr = np.arange(WIN_ROWS)[:, None] + np.arange(WIN_ROWS)[None, :]
    win = tab[:, dr]
    win = jnp.transpose(win, (0, 1, 3, 2, 4)).reshape(nh, WIN_ROWS, GRID_W, WIN_ROWS * GRID_W)
    win = win.reshape(nh // 2, 2, WIN_ROWS, GRID_W, WIN_ROWS * GRID_W)
    return jnp.transpose(win, (0, 2, 1, 3, 4)).reshape(nh // 2, WIN_ROWS, PAIR, WIN_ROWS * GRID_W)


def _block_diag2(w):
    k, c = w.shape[1:]
    z = jnp.zeros((k, c), w.dtype)
    return jnp.concatenate([jnp.concatenate([w[0], z], 1), jnp.concatenate([z, w[1]], 1)], 0)


def _pad_to(w, axis, size):
    pad = [(0, 0)] * w.ndim
    pad[axis] = (0, size - w.shape[axis])
    return jnp.pad(w, pad)


def _seg_matrix():
    i = np.arange(SEG) // HEAD_DIM
    return jnp.asarray(i[:, None] == i[None, :], BF16)


def _pack_rwkv(j, w):
    c = w['rw_rkv'].shape[-1]
    p = {}
    p['mix'] = _pad_to(w['rw_mix'][j], 0, 8)
    p['wrkv'] = w['rw_rkv'][j].astype(BF16)
    dn = [w['rw_w1'][j, 0], w['rw_w1'][j, 1], w['rw_a1'][j, 0], w['rw_a1'][j, 1],
          _pad_to(w['rw_g1'][j], 1, 256)]
    vec = [w['rw_w0'][j, 0], w['rw_w0'][j, 1], w['rw_a0'][j, 0], w['rw_a0'][j, 1],
           w['rw_kk'][j], w['rw_ka'][j], w['rw_rk'][j].reshape(c), jnp.zeros((c,), F32)]
    if j > 0:
        dn.append(_pad_to(w['rw_v1'][j - 1], 1, 128))
        vec.append(w['rw_v0'][j - 1])
        p['v2'] = _pad_to(w['rw_v2'][j - 1], 0, 128).astype(BF16)
    p['dn'] = jnp.concatenate(dn, axis=1).astype(BF16)
    p['vec'] = _pad_to(jnp.stack(vec), 0, 16)
    p['w2'] = _block_diag2(w['rw_w2'][j]).astype(BF16)
    p['a2'] = _block_diag2(w['rw_a2'][j]).astype(BF16)
    p['g2'] = _pad_to(w['rw_g2'][j], 0, 256).astype(BF16)
    p['ln'] = _pad_to(jnp.stack([w['rw_lnx_g'][j], w['rw_lnx_b'][j]]), 0, 8)
    p['wo'] = w['rw_o'][j].astype(BF16)
    return p


def _pack_na(j, w):
    scale = HEAD_DIM ** -0.5 * LOG2E
    return {
        'wqkv': w['na_qkv'][j].astype(BF16),
        'qg': jnp.tile(w['na_q_g'][j] * scale, SEG // HEAD_DIM)[None, :],
        'kg': jnp.tile(w['na_k_g'][j], SEG // HEAD_DIM)[None, :],
        'bias': _na_bias(w['na_rpb'][j]),
        'wo': w['na_o'][j].astype(BF16),
    }


def _trunk(x3, w, depth):
    batch, seq_len, c = x3.shape
    x = x3.reshape(batch * seq_len, c)
    e = _seg_matrix()
    v_first = None
    for layer in range(depth):
        j = layer // 2
        gn = w['norm_mix_g'][layer][None, :]
        mlp = [w['norm_mlp_g'][layer][None, :], w['w_up'][layer].astype(BF16), w['w_down'][layer].astype(BF16)]
        if layer % 2 == 0:
            p = _pack_rwkv(j, w)
            p['gn'], p['e'] = gn, e
            r, v, kk, bv, g, lw0, lw1, kd0, kd1, b0, b1 = _rwkv_pre(x, v_first, p, seq_len, 256)
            if v_first is None:
                v_first = v
            y_f, y_b = _wkv(r, v, kk, lw0, lw1, kd0, kd1, b0, b1, batch, seq_len, WKV_PAIRS, WKV_CHUNKS)
            x = _post(_rwkv_post_kernel, x, [y_f, y_b, bv, g], [p['ln'], e, p['wo']] + mlp, 256)
        else:
            p = _pack_na(j, w)
            p['gn'], p['e'] = gn, e
            q, k, v = _na_qkv(x, p, 512)
            o = _na(q, k, v, p['bias'], batch, seq_len)
            x = _post(_na_post_kernel, x, [o], [p['wo']] + mlp, 512)
    return x.reshape(batch, seq_len, c)


def kernel(x_prompt, x_sample, norm_mix_g, norm_mlp_g, w_up, w_down, rw_mix, rw_rkv, rw_w0, rw_w1, rw_w2,
           rw_a0, rw_a1, rw_a2, rw_v0, rw_v1, rw_v2, rw_g1, rw_g2, rw_kk, rw_ka, rw_rk, rw_lnx_g, rw_lnx_b,
           rw_o, na_qkv, na_q_g, na_k_g, na_rpb, na_o):
    w = dict(norm_mix_g=norm_mix_g, norm_mlp_g=norm_mlp_g, w_up=w_up, w_down=w_down,
             rw_mix=rw_mix, rw_rkv=rw_rkv, rw_w0=rw_w0, rw_w1=rw_w1, rw_w2=rw_w2,
             rw_a0=rw_a0, rw_a1=rw_a1, rw_a2=rw_a2, rw_v0=rw_v0, rw_v1=rw_v1, rw_v2=rw_v2,
             rw_g1=rw_g1, rw_g2=rw_g2, rw_kk=rw_kk, rw_ka=rw_ka, rw_rk=rw_rk,
             rw_lnx_g=rw_lnx_g, rw_lnx_b=rw_lnx_b, rw_o=rw_o,
             na_qkv=na_qkv, na_q_g=na_q_g, na_k_g=na_k_g, na_rpb=na_rpb, na_o=na_o)
    depth = norm_mix_g.shape[0]
    return _trunk(x_prompt, w, depth), _trunk(x_sample, w, depth)
```

```python
import functools

import numpy as np
import jax
import jax.numpy as jnp
from jax import lax
from jax.experimental import pallas as pl
from jax.experimental.pallas import tpu as pltpu

F32 = jnp.float32
BF16 = jnp.bfloat16

HEAD_DIM = 64
PAIR = 2 * HEAD_DIM
SEG = 256
GRID_W = 64
WIN_ROWS = 8
WIN_COLS = 16
RMS_EPS = 1e-6
GN_EPS = 64e-5
L2_EPS = 1e-24
CHUNK = 64
INV_BASE = 16
WKV_PAIRS = 8
WKV_CHUNKS = 4
NA_ROWS = 8
NA_PAIRS = 2
LOG2E = float(np.log2(np.e))
NEG_BIG = -1e30
DECAY_SCALE = float(np.exp(-0.5))
VMEM_LIMIT = 56 * 1024 * 1024


def _dot(a, b):
    return jnp.dot(a, b, preferred_element_type=F32)


def _dot_nt(a, b):
    return lax.dot_general(a, b, (((1,), (1,)), ((), ())), preferred_element_type=F32)


def _split2(x):
    hi = x.astype(BF16)
    lo = (x - hi.astype(F32)).astype(BF16)
    return hi, lo


def _split3(x):
    hi = x.astype(BF16)
    r1 = x - hi.astype(F32)
    mid = r1.astype(BF16)
    lo = (r1 - mid.astype(F32)).astype(BF16)
    return hi, mid, lo


def _segsum(x, e):
    outs = []
    for j in range(x.shape[1] // SEG):
        hi, lo = _split2(x[:, j * SEG:(j + 1) * SEG])
        outs.append(_dot(hi, e) + _dot(lo, e))
    return jnp.concatenate(outs, axis=1) if len(outs) > 1 else outs[0]


def _sigmoid(z):
    return 0.5 * jnp.tanh(0.5 * z) + 0.5


def _rms(x, g):
    return x * lax.rsqrt(jnp.mean(x * x, axis=-1, keepdims=True) + RMS_EPS) * g


def _const_spec(shape):
    nd = len(shape)
    return pl.BlockSpec(shape, lambda *_: (0,) * nd, pipeline_mode=pl.Buffered(1))


def _params(sem):
    return pltpu.CompilerParams(dimension_semantics=sem, vmem_limit_bytes=VMEM_LIMIT)


def _rwkv_pre_kernel(has_vres, seq_tiles, *refs):
    if has_vres:
        (x_ref, xp_ref, xn_ref, vf_ref, gn_ref, mix_ref, wrkv_ref, dn_ref, w2_ref, a2_ref, g2_ref,
         v2_ref, vec_ref, e_ref,
         r_o, v_o, kk_o, bv_o, g_o, lw0_o, lw1_o, kd0_o, kd1_o, b0_o, b1_o) = refs
    else:
        (x_ref, xp_ref, xn_ref, gn_ref, mix_ref, wrkv_ref, dn_ref, w2_ref, a2_ref, g2_ref,
         vec_ref, e_ref,
         r_o, v_o, kk_o, bv_o, g_o, lw0_o, lw1_o, kd0_o, kd1_o, b0_o, b1_o) = refs
    i = pl.program_id(0)
    tm, c = x_ref.shape
    gn = gn_ref[...]
    h = _rms(x_ref[...], gn)
    hp = _rms(xp_ref[7:8, :], gn)
    hn = _rms(xn_ref[0:1, :], gn)
    it = i % seq_tiles
    hp = jnp.where(it == 0, 0.0, hp)
    hn = jnp.where(it == seq_tiles - 1, 0.0, hn)
    row = lax.broadcasted_iota(jnp.int32, (tm, c), 0)
    h_prev = jnp.where(row == 0, hp, pltpu.roll(h, 1, 0))
    h_next = jnp.where(row == tm - 1, hn, pltpu.roll(h, tm - 1, 0))
    xx = 0.5 * (h_prev + h_next) - h

    def mixed(j):
        return (h + xx * mix_ref[j:j + 1, :]).astype(BF16)

    xr, xk, xv = mixed(0), mixed(1), mixed(2)
    wl = jnp.tanh(_dot(mixed(3), dn_ref[:, 0:128])).astype(BF16)
    al = _dot(mixed(4), dn_ref[:, 128:256]).astype(BF16)
    gl = _sigmoid(_dot(mixed(5), dn_ref[:, 256:512])).astype(BF16)
    vl = _dot(xv, dn_ref[:, 512:640]).astype(BF16) if has_vres else None
    e = e_ref[...]

    def project(cols):
        out = dict(
            r=_dot(xr, wrkv_ref[0, :, cols]), k=_dot(xk, wrkv_ref[1, :, cols]), v=_dot(xv, wrkv_ref[2, :, cols]),
            g=_dot(gl, g2_ref[:, cols]),
            w=[_dot(wl, w2_ref[:, d * c + cols.start:d * c + cols.stop]) for d in range(2)],
            a=[_dot(al, a2_ref[:, d * c + cols.start:d * c + cols.stop]) for d in range(2)])
        if has_vres:
            out['vg'] = _dot(vl, v2_ref[:, cols])
        return out

    def finish(cols, q):
        r, k, v = q['r'], q['k'], q['v']
        g_o[:, cols] = q['g']
        if has_vres:
            v = v + (vf_ref[:, cols] - v) * _sigmoid(vec_ref[8:9, cols] + q['vg'])
        v_o[:, cols] = v
        r_o[:, cols] = r
        kk = k * vec_ref[4:5, cols]
        kk = kk * lax.rsqrt(jnp.maximum(_segsum(kk * kk, e), L2_EPS))
        kk_o[:, cols] = kk
        k_a = vec_ref[5:6, cols]
        r_k = vec_ref[6:7, cols]
        bonus_arg = None
        for d, (lw_o, kd_o, b_o) in enumerate(((lw0_o, kd0_o, b0_o), (lw1_o, kd1_o, b1_o))):
            z = vec_ref[d:d + 1, cols] + q['w'][d]
            lw_o[:, cols] = -DECAY_SCALE * _sigmoid(z)
            iclr = _sigmoid(vec_ref[2 + d:3 + d, cols] + q['a'][d])
            k_d = k * (1.0 + (iclr - 1.0) * k_a)
            kd_o[:, cols] = k_d
            b_o[:, cols] = kk * iclr
            term = r * k_d * r_k
            bonus_arg = term if bonus_arg is None else bonus_arg + term
        bv_o[:, cols] = _segsum(bonus_arg, e) * v

    pending = None
    for j in range(c // SEG):
        cols = slice(j * SEG, (j + 1) * SEG)
        q = project(cols)
        if pending is not None:
            finish(*pending)
        pending = (cols, q)
    finish(*pending)


def _rwkv_pre(x, v_first, p, seq_len, tm):
    n, c = x.shape
    has_vres = v_first is not None
    seq_tiles = seq_len // tm
    row_spec = pl.BlockSpec((tm, c), lambda i: (i, 0))
    nblk8 = n // 8
    prev_spec = pl.BlockSpec((8, c), lambda i: (jnp.maximum(i * (tm // 8) - 1, 0), 0))
    next_spec = pl.BlockSpec((8, c), lambda i: (jnp.minimum((i + 1) * (tm // 8), nblk8 - 1), 0))
    ins = [x, x, x]
    specs = [row_spec, prev_spec, next_spec]
    if has_vres:
        ins.append(v_first)
        specs.append(row_spec)
    consts = [p['gn'], p['mix'], p['wrkv'], p['dn'], p['w2'], p['a2'], p['g2']]
    if has_vres:
        consts.append(p['v2'])
    consts += [p['vec'], p['e']]
    ins += consts
    specs += [_const_spec(a.shape) for a in consts]
    out_shape = [jax.ShapeDtypeStruct((n, c), F32)] * 11
    return pl.pallas_call(
        functools.partial(_rwkv_pre_kernel, has_vres, seq_tiles),
        grid=(n // tm,),
        in_specs=specs,
        out_specs=[row_spec] * 11,
        out_shape=out_shape,
        compiler_params=_params(("parallel",)),
        name="rwkv_pre",
    )(*ins)


def _wkv_masks(reverse):
    L = CHUNK
    row = lax.broadcasted_iota(jnp.int32, (L, PAIR), 0)
    lane = lax.broadcasted_iota(jnp.int32, (L, PAIR), 1)
    col = lane & (HEAD_DIM - 1)

    def same_block(n):
        return (col // n) == (row // n)

    offs = []
    size = INV_BASE
    while size < L:
        hi_r = (row & (2 * size - 1)) >= size
        hi_c = (col & (2 * size - 1)) >= size
        offs.append(same_block(2 * size) & ((hi_c & ~hi_r) if reverse else (hi_r & ~hi_c)))
        size *= 2
    ti = lax.broadcasted_iota(jnp.int32, (L, L), 0)
    tj = lax.broadcasted_iota(jnp.int32, (L, L), 1)
    krow = lax.broadcasted_iota(jnp.int32, (PAIR, PAIR), 0)
    vcol = lax.broadcasted_iota(jnp.int32, (PAIR, PAIR), 1)
    return dict(
        head0=lane < HEAD_DIM,
        strict=(col > row) if reverse else (col < row),
        incl=(col >= row) if reverse else (col <= row),
        eye=jnp.where(col == row, 1.0, 0.0),
        base=same_block(INV_BASE),
        offs=offs,
        tri=jnp.where((tj >= ti) if reverse else (tj <= ti), 1.0, 0.0).astype(BF16),
        same_head=(krow // HEAD_DIM) == (vcol // HEAD_DIM),
        last=0 if reverse else L - 1,
        reverse=reverse,
    )


def _bd(q, head0):
    return jnp.concatenate([jnp.where(head0, q, 0.0), jnp.where(head0, 0.0, q)], axis=0).astype(BF16)


def _bdt(q, same_head):
    return jnp.where(same_head, jnp.concatenate([q, q], axis=0).T, 0.0).astype(BF16)


def _wkv_prepare(probs, m):
    L = CHUNK
    head0 = m['head0']
    lcs = []
    for (_, _, _, lw, _, _) in probs:
        both = _dot(m['tri'], jnp.concatenate(_split2(lw), axis=1))
        lcs.append(both[:, :PAIR] + both[:, PAIR:])
    xs, yts, zts, dcs = [], [], [], []
    for (r, v, kk, lw, kd, b), lc in zip(probs, lcs):
        lc_end = lc[m['last']:m['last'] + 1, :]
        e_inv = jnp.exp(-lc)
        e_rem = jnp.exp(lc_end - lc)
        a_t = -(kk * jnp.exp(lc - lw))
        r_t = r * jnp.exp(lc)
        xs.append(jnp.concatenate([a_t, r_t], axis=0).astype(BF16))
        yts.append(jnp.concatenate([_bdt(b * e_inv, m['same_head']), _bdt(kd * e_inv, m['same_head'])],
                                   axis=1))
        z = jnp.concatenate([b * e_rem, kd * e_rem], axis=0)
        zts.append(z.T.astype(BF16))
        dcs.append(jnp.broadcast_to(jnp.exp(lc_end), (PAIR, PAIR)).T)
    scs = [_dot(x, yt) for x, yt in zip(xs, yts)]
    abs_ = [jnp.where(m['strict'], sc[:L, :PAIR], 0.0) for sc in scs]
    akrk = [jnp.concatenate([jnp.where(m['strict'], sc[:L, PAIR:], 0.0),
                             jnp.where(m['incl'], sc[L:, PAIR:], 0.0)], axis=0).astype(BF16) for sc in scs]
    rbs = [jnp.where(m['incl'], sc[L:, :PAIR], 0.0).astype(BF16) for sc in scs]
    xvs = [_dot(a, _bd(p[1], head0)) for a, p in zip(akrk, probs)]

    pws = [jnp.where(m['base'], ab, 0.0) for ab in abs_]
    ts = [m['eye'] + pw for pw in pws]
    pws = [_dot(pw.astype(BF16), _bd(pw, head0)) for pw in pws]
    n = 2
    while n < INV_BASE:
        if 2 * n < INV_BASE:
            both = [_dot(jnp.concatenate([pw, t], axis=0).astype(BF16), _bd(pw, head0)) for pw, t in zip(pws, ts)]
            pws = [bt[:L] for bt in both]
            ts = [t + bt[L:] for t, bt in zip(ts, both)]
        else:
            ts = [t + _dot(t.astype(BF16), _bd(pw, head0)) for t, pw in zip(ts, pws)]
        n *= 2
    size = INV_BASE
    for off in m['offs']:
        blocks = [(g * size, (g + 1) * size) for g in range(L // size)]
        active = [(g % 2 == 1) != m['reverse'] for g in range(L // size)]

        def moving(t):
            return jnp.concatenate([t[a:b] for (a, b), act in zip(blocks, active) if act], axis=0)

        def merged(t, delta):
            pieces, k = [], 0
            for (a, b), act in zip(blocks, active):
                pieces.append(t[a:b] + delta[k * size:(k + 1) * size] if act else t[a:b])
                k += act
            return jnp.concatenate(pieces, axis=0)

        tas = [_dot(moving(t).astype(BF16), _bd(jnp.where(off, ab, 0.0), head0)) for t, ab in zip(ts, abs_)]
        ts = [merged(t, _dot(ta.astype(BF16), _bd(t, head0))) for t, ta in zip(ts, tas)]
        size *= 2
    return [dict(x=x, t=t.astype(BF16), xv=xv, rb=rb, zt=zt, dc=dc, v=p[1])
            for x, t, xv, rb, zt, dc, p in zip(xs, ts, xvs, rbs, zts, dcs, probs)]


def _wkv_advance(preps, states, m):
    L = CHUNK
    head0 = m['head0']
    xss = [_dot(q['x'], s.astype(BF16)) for q, s in zip(preps, states)]
    us = [_dot(q['t'], _bd(xs[:L] + q['xv'][:L], head0)) for q, xs in zip(preps, xss)]
    ys = [xs[L:] + q['xv'][L:] + _dot(q['rb'], _bd(u, head0)) for q, xs, u in zip(preps, xss, us)]
    upds = [_dot(q['zt'], jnp.concatenate([u, q['v']], axis=0).astype(BF16)) for q, u in zip(preps, us)]
    new = [s * q['dc'] + jnp.where(m['same_head'], upd, 0.0) for q, s, upd in zip(preps, states, upds)]
    return ys, new


def _wkv_kernel(npairs, nchunks, *refs):
    fwd = refs[0:6]
    bwd = refs[6:12]
    y_f, y_b, s_f, s_b = refs[12:16]

    @pl.when(pl.program_id(2) == 0)
    def _():
        s_f[...] = jnp.zeros_like(s_f)
        s_b[...] = jnp.zeros_like(s_b)

    for ins, y_ref, s_ref, reverse in ((fwd, y_f, s_f, False), (bwd, y_b, s_b, True)):
        m = _wkv_masks(reverse)
        probs = []
        for ck in range(nchunks):
            for p in range(npairs):
                probs.append(tuple(ref[ck * CHUNK:(ck + 1) * CHUNK, p * PAIR:(p + 1) * PAIR] for ref in ins))
        preps = _wkv_prepare(probs, m)
        states = [s_ref[p] for p in range(npairs)]
        for ck in (reversed(range(nchunks)) if reverse else range(nchunks)):
            ys, states = _wkv_advance(preps[ck * npairs:(ck + 1) * npairs], states, m)
            for p, y in enumerate(ys):
                y_ref[ck * CHUNK:(ck + 1) * CHUNK, p * PAIR:(p + 1) * PAIR] = y
        for p, s in enumerate(states):
            s_ref[p] = s


def _wkv(r, v, kk, lw0, lw1, kd0, kd1, b0, b1, batch, seq_len, npairs, nchunks):
    n, c = r.shape
    rows = nchunks * CHUNK
    nb = seq_len // rows
    width = npairs * PAIR
    fwd_spec = pl.BlockSpec((rows, width), lambda bi, hg, ci: (bi * nb + ci, hg))
    bwd_spec = pl.BlockSpec((rows, width), lambda bi, hg, ci: (bi * nb + nb - 1 - ci, hg))
    ins = [r, v, kk, lw0, kd0, b0, r, v, kk, lw1, kd1, b1]
    return pl.pallas_call(
        functools.partial(_wkv_kernel, npairs, nchunks),
        grid=(batch, c // width, nb),
        in_specs=[fwd_spec] * 6 + [bwd_spec] * 6,
        out_specs=[fwd_spec, bwd_spec],
        out_shape=[jax.ShapeDtypeStruct((n, c), F32)] * 2,
        scratch_shapes=[pltpu.VMEM((npairs, PAIR, PAIR), F32)] * 2,
        compiler_params=_params(("parallel", "parallel", "arbitrary")),
        name="wkv_scan",
    )(*ins)


def _mlp_tail(x1, gm_ref, wup_ref, wdn_ref, out_ref):
    hb = _rms(x1, gm_ref[...]).astype(BF16)
    c = x1.shape[1]
    acc = x1
    nblk = wup_ref.shape[1] // c
    u = _dot(hb, wup_ref[:, 0:c])
    for j in range(nblk):
        u_next = _dot(hb, wup_ref[:, (j + 1) * c:(j + 2) * c]) if j + 1 < nblk else None
        u = jnp.maximum(u, 0.0)
        acc = acc + _dot((u * u).astype(BF16), wdn_ref[j * c:(j + 1) * c, :])
        u = u_next
    out_ref[...] = acc


def _na_post_kernel(x_ref, o_ref, wo_ref, gm_ref, wup_ref, wdn_ref, out_ref):
    x1 = x_ref[...] + _dot(o_ref[...], wo_ref[...])
    _mlp_tail(x1, gm_ref, wup_ref, wdn_ref, out_ref)


def _rwkv_post_kernel(x_ref, yf_ref, yb_ref, bv_ref, g_ref, ln_ref, e_ref, wo_ref, gm_ref, wup_ref,
                      wdn_ref, out_ref):
    e = e_ref[...]
    y = yf_ref[...] + yb_ref[...]
    inv_n = 1.0 / HEAD_DIM
    d = y - _segsum(y, e) * inv_n
    var = _segsum(d * d, e) * inv_n
    yn = d * lax.rsqrt(var + GN_EPS) * ln_ref[0:1, :] + ln_ref[1:2, :] + bv_ref[...]
    x1 = x_ref[...] + _dot((yn * g_ref[...]).astype(BF16), wo_ref[...])
    _mlp_tail(x1, gm_ref, wup_ref, wdn_ref, out_ref)


def _post(kernel, x, acts, consts, tm):
    n, c = x.shape
    row_spec = pl.BlockSpec((tm, c), lambda i: (i, 0))
    ins = [x] + list(acts) + list(consts)
    specs = [row_spec] * (1 + len(acts)) + [_const_spec(a.shape) for a in consts]
    return pl.pallas_call(
        kernel,
        grid=(n // tm,),
        in_specs=specs,
        out_specs=row_spec,
        out_shape=jax.ShapeDtypeStruct((n, c), F32),
        compiler_params=_params(("parallel",)),
        name=kernel.__name__.strip("_"),
    )(*ins)


def _na_qkv_kernel(x_ref, gn_ref, w_ref, qg_ref, kg_ref, e_ref, q_o, k_o, v_o):
    hb = _rms(x_ref[...], gn_ref[...]).astype(BF16)
    c = x_ref.shape[1]
    e = e_ref[...]
    inv_n = 1.0 / HEAD_DIM
    blocks = [(out, gain, part * c + j * SEG, j * SEG)
              for part, (out, gain) in enumerate(((q_o, qg_ref), (k_o, kg_ref), (v_o, None)))
              for j in range(c // SEG)]

    def finish(acc, out, gain, col):
        if gain is not None:
            ms = _segsum(acc * acc, e) * inv_n
            acc = acc * lax.rsqrt(ms + RMS_EPS) * gain[...]
        out[:, col:col + SEG] = acc.astype(out.dtype)

    pending = None
    for out, gain, wcol, col in blocks:
        acc = _dot(hb, w_ref[:, wcol:wcol + SEG])
        if pending is not None:
            finish(*pending)
        pending = (acc, out, gain, col)
    finish(*pending)


def _na_qkv(x, p, tm):
    n, c = x.shape
    row_spec = pl.BlockSpec((tm, c), lambda i: (i, 0))
    consts = [p['gn'], p['wqkv'], p['qg'], p['kg'], p['e']]
    return pl.pallas_call(
        _na_qkv_kernel,
        grid=(n // tm,),
        in_specs=[row_spec] + [_const_spec(a.shape) for a in consts],
        out_specs=[row_spec] * 3,
        out_shape=[jax.ShapeDtypeStruct((n, c), BF16)] * 3,
        compiler_params=_params(("parallel",)),
        name="na_qkv",
    )(x, *consts)


def _na_kernel(rows, q_ref, kp_ref, kc_ref, kn_ref, vp_ref, vc_ref, vn_ref, bias_ref, o_ref, kw_ref, vw_ref):
    ib = pl.program_id(2)
    blk = NA_ROWS * GRID_W
    for j, (kr, vr) in enumerate(((kp_ref, vp_ref), (kc_ref, vc_ref), (kn_ref, vn_ref))):
        kw_ref[j * blk:(j + 1) * blk, :] = kr[...]
        vw_ref[j * blk:(j + 1) * blk, :] = vr[...]
    lane = lax.broadcasted_iota(jnp.int32, (GRID_W, PAIR), 1)
    head0 = lane < HEAD_DIM
    nkeys = WIN_ROWS * GRID_W
    probs = []
    for p in range(NA_PAIRS):
        lanes = slice(p * PAIR, (p + 1) * PAIR)
        for ii in range(NA_ROWS):
            i = ib * NA_ROWS + ii
            r0 = jnp.clip(i - WIN_ROWS // 2, 0, rows - WIN_ROWS)
            start = pl.multiple_of((r0 - ib * NA_ROWS + NA_ROWS) * GRID_W, GRID_W)
            probs.append((p, ii, lanes, pl.ds(start, nkeys), r0 - i + WIN_ROWS - 1))
    qss = []
    for p, ii, lanes, _, _ in probs:
        q = q_ref[ii * GRID_W:(ii + 1) * GRID_W, lanes]
        zero = jnp.zeros_like(q)
        qss.append(jnp.concatenate([jnp.where(head0, q, zero), jnp.where(head0, zero, q)], axis=0))
    ss = [_dot_nt(qs, kw_ref[win, lanes]) + bias_ref[p, d0] for qs, (p, _, lanes, win, d0) in zip(qss, probs)]
    prs = [jnp.exp2(s - jnp.max(s, axis=-1, keepdims=True)) for s in ss]
    inv_ls = [1.0 / jnp.sum(pr, axis=-1, keepdims=True) for pr in prs]
    pvs = [_dot(pr.astype(BF16), vw_ref[win, lanes]) * inv_l
           for pr, inv_l, (_, _, lanes, win, _) in zip(prs, inv_ls, probs)]
    for pv, (_, ii, lanes, _, _) in zip(pvs, probs):
        o = jnp.where(head0, pv[:GRID_W], pv[GRID_W:])
        o_ref[ii * GRID_W:(ii + 1) * GRID_W, lanes] = o.astype(o_ref.dtype)


def _na(q, k, v, bias, batch, seq_len):
    n, c = q.shape
    rows = seq_len // GRID_W
    nrb = rows // NA_ROWS
    blk = NA_ROWS * GRID_W
    width = NA_PAIRS * PAIR

    def spec(off):
        return pl.BlockSpec((blk, width), lambda p, bi, ib: (bi * nrb + jnp.clip(ib + off, 0, nrb - 1), p))

    bias_spec = pl.BlockSpec((NA_PAIRS,) + bias.shape[1:], lambda p, bi, ib: (p, 0, 0, 0))
    return pl.pallas_call(
        functools.partial(_na_kernel, rows),
        grid=(c // width, batch, nrb),
        in_specs=[spec(0), spec(-1), spec(0), spec(1), spec(-1), spec(0), spec(1), bias_spec],
        out_specs=spec(0),
        out_shape=jax.ShapeDtypeStruct((n, c), BF16),
        scratch_shapes=[pltpu.VMEM((3 * blk, width), BF16)] * 2,
        compiler_params=_params(("parallel", "parallel", "arbitrary")),
        name="na_attn",
    )(q, k, k, k, v, v, v, bias)


def _na_bias(rpb):
    nh = rpb.shape[0]
    qc = np.arange(GRID_W)[:, None]
    kc = np.arange(GRID_W)[None, :]
    win_start = np.clip(qc - WIN_COLS // 2, 0, GRID_W - WIN_COLS)
    valid = (kc >= win_start) & (kc < win_start + WIN_COLS)
    dc = np.clip(kc - qc + WIN_COLS - 1, 0, 2 * WIN_COLS - 2)
    tab = jnp.where(valid[None, None], rpb[:, :, dc] * LOG2E, NEG_BIG)
    dr = np.arange(WIN_ROWS)[:, None] + np.arange(WIN_ROWS)[None, :]
    win = tab[:, dr]
    win = jnp.transpose(win, (0, 1, 3, 2, 4)).reshape(nh, WIN_ROWS, GRID_W, WIN_ROWS * GRID_W)
    win = win.reshape(nh // 2, 2, WIN_ROWS, GRID_W, WIN_ROWS * GRID_W)
    return jnp.transpose(win, (0, 2, 1, 3, 4)).reshape(nh // 2, WIN_ROWS, PAIR, WIN_ROWS * GRID_W)


def _block_diag2(w):
    k, c = w.shape[1:]
    z = jnp.zeros((k, c), w.dtype)
    return jnp.concatenate([jnp.concatenate([w[0], z], 1), jnp.concatenate([z, w[1]], 1)], 0)


def _pad_to(w, axis, size):
    pad = [(0, 0)] * w.ndim
    pad[axis] = (0, size - w.shape[axis])
    return jnp.pad(w, pad)


def _seg_matrix():
    i = np.arange(SEG) // HEAD_DIM
    return jnp.asarray(i[:, None] == i[None, :], BF16)


def _pack_rwkv(j, w):
    c = w['rw_rkv'].shape[-1]
    p = {}
    p['mix'] = _pad_to(w['rw_mix'][j], 0, 8)
    p['wrkv'] = w['rw_rkv'][j].astype(BF16)
    dn = [w['rw_w1'][j, 0], w['rw_w1'][j, 1], w['rw_a1'][j, 0], w['rw_a1'][j, 1],
          _pad_to(w['rw_g1'][j], 1, 256)]
    vec = [w['rw_w0'][j, 0], w['rw_w0'][j, 1], w['rw_a0'][j, 0], w['rw_a0'][j, 1],
           w['rw_kk'][j], w['rw_ka'][j], w['rw_rk'][j].reshape(c), jnp.zeros((c,), F32)]
    if j > 0:
        dn.append(_pad_to(w['rw_v1'][j - 1], 1, 128))
        vec.append(w['rw_v0'][j - 1])
        p['v2'] = _pad_to(w['rw_v2'][j - 1], 0, 128).astype(BF16)
    p['dn'] = jnp.concatenate(dn, axis=1).astype(BF16)
    p['vec'] = _pad_to(jnp.stack(vec), 0, 16)
    p['w2'] = _block_diag2(w['rw_w2'][j]).astype(BF16)
    p['a2'] = _block_diag2(w['rw_a2'][j]).astype(BF16)
    p['g2'] = _pad_to(w['rw_g2'][j], 0, 256).astype(BF16)
    p['ln'] = _pad_to(jnp.stack([w['rw_lnx_g'][j], w['rw_lnx_b'][j]]), 0, 8)
    p['wo'] = w['rw_o'][j].astype(BF16)
    return p


def _pack_na(j, w):
    scale = HEAD_DIM ** -0.5 * LOG2E
    return {
        'wqkv': w['na_qkv'][j].astype(BF16),
        'qg': jnp.tile(w['na_q_g'][j] * scale, SEG // HEAD_DIM)[None, :],
        'kg': jnp.tile(w['na_k_g'][j], SEG // HEAD_DIM)[None, :],
        'bias': _na_bias(w['na_rpb'][j]),
        'wo': w['na_o'][j].astype(BF16),
    }


def _trunk(x3, w, depth):
    batch, seq_len, c = x3.shape
    x = x3.reshape(batch * seq_len, c)
    e = _seg_matrix()
    v_first = None
    for layer in range(depth):
        j = layer // 2
        gn = w['norm_mix_g'][layer][None, :]
        mlp = [w['norm_mlp_g'][layer][None, :], w['w_up'][layer].astype(BF16), w['w_down'][layer].astype(BF16)]
        if layer % 2 == 0:
            p = _pack_rwkv(j, w)
            p['gn'], p['e'] = gn, e
            r, v, kk, bv, g, lw0, lw1, kd0, kd1, b0, b1 = _rwkv_pre(x, v_first, p, seq_len, 256)
            if v_first is None:
                v_first = v
            y_f, y_b = _wkv(r, v, kk, lw0, lw1, kd0, kd1, b0, b1, batch, seq_len, WKV_PAIRS, WKV_CHUNKS)
            x = _post(_rwkv_post_kernel, x, [y_f, y_b, bv, g], [p['ln'], e, p['wo']] + mlp, 256)
        else:
            p = _pack_na(j, w)
            p['gn'], p['e'] = gn, e
            q, k, v = _na_qkv(x, p, 512)
            o = _na(q, k, v, p['bias'], batch, seq_len)
            x = _post(_na_post_kernel, x, [o], [p['wo']] + mlp, 512)
    return x.reshape(batch, seq_len, c)


def kernel(x_prompt, x_sample, norm_mix_g, norm_mlp_g, w_up, w_down, rw_mix, rw_rkv, rw_w0, rw_w1, rw_w2,
           rw_a0, rw_a1, rw_a2, rw_v0, rw_v1, rw_v2, rw_g1, rw_g2, rw_kk, rw_ka, rw_rk, rw_lnx_g, rw_lnx_b,
           rw_o, na_qkv, na_q_g, na_k_g, na_rpb, na_o):
    w = dict(norm_mix_g=norm_mix_g, norm_mlp_g=norm_mlp_g, w_up=w_up, w_down=w_down,
             rw_mix=rw_mix, rw_rkv=rw_rkv, rw_w0=rw_w0, rw_w1=rw_w1, rw_w2=rw_w2,
             rw_a0=rw_a0, rw_a1=rw_a1, rw_a2=rw_a2, rw_v0=rw_v0, rw_v1=rw_v1, rw_v2=rw_v2,
             rw_g1=rw_g1, rw_g2=rw_g2, rw_kk=rw_kk, rw_ka=rw_ka, rw_rk=rw_rk,
             rw_lnx_g=rw_lnx_g, rw_lnx_b=rw_lnx_b, rw_o=rw_o,
             na_qkv=na_qkv, na_q_g=na_q_g, na_k_g=na_k_g, na_rpb=na_rpb, na_o=na_o)
    depth = norm_mix_g.shape[0]
    return _trunk(x_prompt, w, depth), _trunk(x_sample, w, depth)
```

```python
import functools

import numpy as np
import jax
import jax.numpy as jnp
from jax import lax
from jax.experimental import pallas as pl
from jax.experimental.pallas import tpu as pltpu

F32 = jnp.float32
BF16 = jnp.bfloat16

HEAD_DIM = 64
PAIR = 2 * HEAD_DIM
SEG = 256
GRID_W = 64
WIN_ROWS = 8
WIN_COLS = 16
RMS_EPS = 1e-6
GN_EPS = 64e-5
L2_EPS = 1e-24
CHUNK = 64
INV_BASE = 16
WKV_PAIRS = 8
WKV_CHUNKS = 4
NA_ROWS = 8
NA_PAIRS = 2
LOG2E = float(np.log2(np.e))
NEG_BIG = -1e30
DECAY_SCALE = float(np.exp(-0.5))
VMEM_LIMIT = 56 * 1024 * 1024


def _dot(a, b):
    return jnp.dot(a, b, preferred_element_type=F32)


def _dot_nt(a, b):
    return lax.dot_general(a, b, (((1,), (1,)), ((), ())), preferred_element_type=F32)


def _split2(x):
    hi = x.astype(BF16)
    lo = (x - hi.astype(F32)).astype(BF16)
    return hi, lo


def _split3(x):
    hi = x.astype(BF16)
    r1 = x - hi.astype(F32)
    mid = r1.astype(BF16)
    lo = (r1 - mid.astype(F32)).astype(BF16)
    return hi, mid, lo


def _segsum(x, e):
    outs = []
    for j in range(x.shape[1] // SEG):
        hi, lo = _split2(x[:, j * SEG:(j + 1) * SEG])
        outs.append(_dot(hi, e) + _dot(lo, e))
    return jnp.concatenate(outs, axis=1) if len(outs) > 1 else outs[0]


def _rms(x, g):
    return x * lax.rsqrt(jnp.mean(x * x, axis=-1, keepdims=True) + RMS_EPS) * g


def _const_spec(shape):
    nd = len(shape)
    return pl.BlockSpec(shape, lambda *_: (0,) * nd, pipeline_mode=pl.Buffered(1))


def _params(sem):
    return pltpu.CompilerParams(dimension_semantics=sem, vmem_limit_bytes=VMEM_LIMIT)


def _rwkv_pre_kernel(has_vres, seq_tiles, *refs):
    if has_vres:
        (x_ref, xp_ref, xn_ref, vf_ref, gn_ref, mix_ref, wrkv_ref, dn_ref, w2_ref, a2_ref, g2_ref,
         v2_ref, vec_ref, e_ref,
         r_o, v_o, kk_o, bv_o, g_o, lw0_o, lw1_o, kd0_o, kd1_o, b0_o, b1_o) = refs
    else:
        (x_ref, xp_ref, xn_ref, gn_ref, mix_ref, wrkv_ref, dn_ref, w2_ref, a2_ref, g2_ref,
         vec_ref, e_ref,
         r_o, v_o, kk_o, bv_o, g_o, lw0_o, lw1_o, kd0_o, kd1_o, b0_o, b1_o) = refs
    i = pl.program_id(0)
    tm, c = x_ref.shape
    gn = gn_ref[...]
    h = _rms(x_ref[...], gn)
    hp = _rms(xp_ref[7:8, :], gn)
    hn = _rms(xn_ref[0:1, :], gn)
    it = i % seq_tiles
    hp = jnp.where(it == 0, 0.0, hp)
    hn = jnp.where(it == seq_tiles - 1, 0.0, hn)
    row = lax.broadcasted_iota(jnp.int32, (tm, c), 0)
    h_prev = jnp.where(row == 0, hp, pltpu.roll(h, 1, 0))
    h_next = jnp.where(row == tm - 1, hn, pltpu.roll(h, tm - 1, 0))
    xx = 0.5 * (h_prev + h_next) - h

    def mixed(j):
        return (h + xx * mix_ref[j:j + 1, :]).astype(BF16)

    xr, xk, xv = mixed(0), mixed(1), mixed(2)
    wl = jnp.tanh(_dot(mixed(3), dn_ref[:, 0:128])).astype(BF16)
    al = _dot(mixed(4), dn_ref[:, 128:256]).astype(BF16)
    gl = (0.5 * jnp.tanh(_dot(mixed(5), dn_ref[:, 256:512])) + 0.5).astype(BF16)
    vl = _dot(xv, dn_ref[:, 512:640]).astype(BF16) if has_vres else None
    e = e_ref[...]

    def project(cols):
        out = dict(
            r=_dot(xr, wrkv_ref[0, :, cols]), k=_dot(xk, wrkv_ref[1, :, cols]), v=_dot(xv, wrkv_ref[2, :, cols]),
            g=_dot(gl, g2_ref[:, cols]),
            w=[_dot(wl, w2_ref[:, d * c + cols.start:d * c + cols.stop]) for d in range(2)],
            a=[_dot(al, a2_ref[:, d * c + cols.start:d * c + cols.stop]) for d in range(2)])
        if has_vres:
            out['vg'] = _dot(vl, v2_ref[:, cols])
        return out

    def finish(cols, q):
        r, k, v = q['r'], q['k'], q['v']
        g_o[:, cols] = q['g']
        if has_vres:
            v = v + (vf_ref[:, cols] - v) * (0.5 * jnp.tanh(vec_ref[8:9, cols] + q['vg']) + 0.5)
        v_o[:, cols] = v
        r_o[:, cols] = r
        kk = k * vec_ref[4:5, cols]
        kk = kk * lax.rsqrt(jnp.maximum(_segsum(kk * kk, e), L2_EPS))
        kk_o[:, cols] = kk
        kk_half = 0.5 * kk
        r_rk = r * vec_ref[6:7, cols]
        bonus_arg = None
        for d, (lw_o, kd_o, b_o) in enumerate(((lw0_o, kd0_o, b0_o), (lw1_o, kd1_o, b1_o))):
            lw_o[:, cols] = (-0.5 * DECAY_SCALE) * jnp.tanh(vec_ref[d:d + 1, cols] + q['w'][d]) - 0.5 * DECAY_SCALE
            ta = jnp.tanh(vec_ref[2 + d:3 + d, cols] + q['a'][d])
            k_d = k * (vec_ref[5:6, cols] + vec_ref[7:8, cols] * ta)
            kd_o[:, cols] = k_d
            b_o[:, cols] = kk_half * ta + kk_half
            term = r_rk * k_d
            bonus_arg = term if bonus_arg is None else bonus_arg + term
        bv_o[:, cols] = _segsum(bonus_arg, e) * v

    pending = None
    for j in range(c // SEG):
        cols = slice(j * SEG, (j + 1) * SEG)
        q = project(cols)
        if pending is not None:
            finish(*pending)
        pending = (cols, q)
    finish(*pending)


def _rwkv_pre(x, v_first, p, seq_len, tm):
    n, c = x.shape
    has_vres = v_first is not None
    seq_tiles = seq_len // tm
    row_spec = pl.BlockSpec((tm, c), lambda i: (i, 0))
    nblk8 = n // 8
    prev_spec = pl.BlockSpec((8, c), lambda i: (jnp.maximum(i * (tm // 8) - 1, 0), 0))
    next_spec = pl.BlockSpec((8, c), lambda i: (jnp.minimum((i + 1) * (tm // 8), nblk8 - 1), 0))
    ins = [x, x, x]
    specs = [row_spec, prev_spec, next_spec]
    if has_vres:
        ins.append(v_first)
        specs.append(row_spec)
    consts = [p['gn'], p['mix'], p['wrkv'], p['dn'], p['w2'], p['a2'], p['g2']]
    if has_vres:
        consts.append(p['v2'])
    consts += [p['vec'], p['e']]
    ins += consts
    specs += [_const_spec(a.shape) for a in consts]
    out_shape = [jax.ShapeDtypeStruct((n, c), F32)] * 11
    return pl.pallas_call(
        functools.partial(_rwkv_pre_kernel, has_vres, seq_tiles),
        grid=(n // tm,),
        in_specs=specs,
        out_specs=[row_spec] * 11,
        out_shape=out_shape,
        compiler_params=_params(("parallel",)),
        name="rwkv_pre",
    )(*ins)


def _wkv_masks(reverse):
    L = CHUNK
    row = lax.broadcasted_iota(jnp.int32, (L, PAIR), 0)
    lane = lax.broadcasted_iota(jnp.int32, (L, PAIR), 1)
    col = lane & (HEAD_DIM - 1)

    def same_block(n):
        return (col // n) == (row // n)

    offs = []
    size = INV_BASE
    while size < L:
        hi_r = (row & (2 * size - 1)) >= size
        hi_c = (col & (2 * size - 1)) >= size
        offs.append(same_block(2 * size) & ((hi_c & ~hi_r) if reverse else (hi_r & ~hi_c)))
        size *= 2
    ti = lax.broadcasted_iota(jnp.int32, (L, L), 0)
    tj = lax.broadcasted_iota(jnp.int32, (L, L), 1)
    krow = lax.broadcasted_iota(jnp.int32, (PAIR, PAIR), 0)
    vcol = lax.broadcasted_iota(jnp.int32, (PAIR, PAIR), 1)
    return dict(
        head0=lane < HEAD_DIM,
        strict=(col > row) if reverse else (col < row),
        incl=(col >= row) if reverse else (col <= row),
        eye=jnp.where(col == row, 1.0, 0.0),
        base=same_block(INV_BASE),
        offs=offs,
        tri=jnp.where((tj >= ti) if reverse else (tj <= ti), 1.0, 0.0).astype(BF16),
        same_head=(krow // HEAD_DIM) == (vcol // HEAD_DIM),
        last=0 if reverse else L - 1,
        reverse=reverse,
    )


def _bd(q, head0):
    return jnp.concatenate([jnp.where(head0, q, 0.0), jnp.where(head0, 0.0, q)], axis=0).astype(BF16)


def _bdt(q, same_head):
    return jnp.where(same_head, jnp.concatenate([q, q], axis=0).T, 0.0).astype(BF16)


def _wkv_prepare(probs, m):
    L = CHUNK
    head0 = m['head0']
    lcs = []
    for (_, _, _, lw, _, _) in probs:
        both = _dot(m['tri'], jnp.concatenate(_split2(lw), axis=1))
        lcs.append(both[:, :PAIR] + both[:, PAIR:])
    xs, yts, zts, dcs = [], [], [], []
    for (r, v, kk, lw, kd, b), lc in zip(probs, lcs):
        lc_end = lc[m['last']:m['last'] + 1, :]
        e_inv = jnp.exp(-lc)
        e_rem = jnp.exp(lc_end - lc)
        a_t = -(kk * jnp.exp(lc - lw))
        r_t = r * jnp.exp(lc)
        xs.append(jnp.concatenate([a_t, r_t], axis=0).astype(BF16))
        yts.append(jnp.concatenate([_bdt(b * e_inv, m['same_head']), _bdt(kd * e_inv, m['same_head'])],
                                   axis=1))
        z = jnp.concatenate([b * e_rem, kd * e_rem], axis=0)
        zts.append(z.T.astype(BF16))
        dcs.append(jnp.broadcast_to(jnp.exp(lc_end), (PAIR, PAIR)).T)
    scs = [_dot(x, yt) for x, yt in zip(xs, yts)]
    abs_ = [jnp.where(m['strict'], sc[:L, :PAIR], 0.0) for sc in scs]
    akrk = [jnp.concatenate([jnp.where(m['strict'], sc[:L, PAIR:], 0.0),
                             jnp.where(m['incl'], sc[L:, PAIR:], 0.0)], axis=0).astype(BF16) for sc in scs]
    rbs = [jnp.where(m['incl'], sc[L:, :PAIR], 0.0).astype(BF16) for sc in scs]
    xvs = [_dot(a, _bd(p[1], head0)) for a, p in zip(akrk, probs)]

    pws = [jnp.where(m['base'], ab, 0.0) for ab in abs_]
    ts = [m['eye'] + pw for pw in pws]
    pws = [_dot(pw.astype(BF16), _bd(pw, head0)) for pw in pws]
    n = 2
    while n < INV_BASE:
        if 2 * n < INV_BASE:
            both = [_dot(jnp.concatenate([pw, t], axis=0).astype(BF16), _bd(pw, head0)) for pw, t in zip(pws, ts)]
            pws = [bt[:L] for bt in both]
            ts = [t + bt[L:] for t, bt in zip(ts, both)]
        else:
            ts = [t + _dot(t.astype(BF16), _bd(pw, head0)) for t, pw in zip(ts, pws)]
        n *= 2
    size = INV_BASE
    for off in m['offs']:
        blocks = [(g * size, (g + 1) * size) for g in range(L // size)]
        active = [(g % 2 == 1) != m['reverse'] for g in range(L // size)]

        def moving(t):
            return jnp.concatenate([t[a:b] for (a, b), act in zip(blocks, active) if act], axis=0)

        def merged(t, delta):
            pieces, k = [], 0
            for (a, b), act in zip(blocks, active):
                pieces.append(t[a:b] + delta[k * size:(k + 1) * size] if act else t[a:b])
                k += act
            return jnp.concatenate(pieces, axis=0)

        tas = [_dot(moving(t).astype(BF16), _bd(jnp.where(off, ab, 0.0), head0)) for t, ab in zip(ts, abs_)]
        ts = [merged(t, _dot(ta.astype(BF16), _bd(t, head0))) for t, ta in zip(ts, tas)]
        size *= 2
    return [dict(x=x, t=t.astype(BF16), xv=xv, rb=rb, zt=zt, dc=dc, v=p[1])
            for x, t, xv, rb, zt, dc, p in zip(xs, ts, xvs, rbs, zts, dcs, probs)]


def _wkv_advance(preps, states, m):
    L = CHUNK
    head0 = m['head0']
    xss = [_dot(q['x'], s.astype(BF16)) for q, s in zip(preps, states)]
    us = [_dot(q['t'], _bd(xs[:L] + q['xv'][:L], head0)) for q, xs in zip(preps, xss)]
    ys = [xs[L:] + q['xv'][L:] + _dot(q['rb'], _bd(u, head0)) for q, xs, u in zip(preps, xss, us)]
    upds = [_dot(q['zt'], jnp.concatenate([u, q['v']], axis=0).astype(BF16)) for q, u in zip(preps, us)]
    new = [s * q['dc'] + jnp.where(m['same_head'], upd, 0.0) for q, s, upd in zip(preps, states, upds)]
    return ys, new


def _wkv_kernel(npairs, nchunks, *refs):
    fwd = refs[0:6]
    bwd = refs[6:12]
    y_f, y_b, s_f, s_b = refs[12:16]

    @pl.when(pl.program_id(2) == 0)
    def _():
        s_f[...] = jnp.zeros_like(s_f)
        s_b[...] = jnp.zeros_like(s_b)

    for ins, y_ref, s_ref, reverse in ((fwd, y_f, s_f, False), (bwd, y_b, s_b, True)):
        m = _wkv_masks(reverse)
        probs = []
        for ck in range(nchunks):
            for p in range(npairs):
                probs.append(tuple(ref[ck * CHUNK:(ck + 1) * CHUNK, p * PAIR:(p + 1) * PAIR] for ref in ins))
        preps = _wkv_prepare(probs, m)
        states = [s_ref[p] for p in range(npairs)]
        for ck in (reversed(range(nchunks)) if reverse else range(nchunks)):
            ys, states = _wkv_advance(preps[ck * npairs:(ck + 1) * npairs], states, m)
            for p, y in enumerate(ys):
                y_ref[ck * CHUNK:(ck + 1) * CHUNK, p * PAIR:(p + 1) * PAIR] = y
        for p, s in enumerate(states):
            s_ref[p] = s


def _wkv(r, v, kk, lw0, lw1, kd0, kd1, b0, b1, batch, seq_len, npairs, nchunks):
    n, c = r.shape
    rows = nchunks * CHUNK
    nb = seq_len // rows
    width = npairs * PAIR
    fwd_spec = pl.BlockSpec((rows, width), lambda bi, hg, ci: (bi * nb + ci, hg))
    bwd_spec = pl.BlockSpec((rows, width), lambda bi, hg, ci: (bi * nb + nb - 1 - ci, hg))
    ins = [r, v, kk, lw0, kd0, b0, r, v, kk, lw1, kd1, b1]
    return pl.pallas_call(
        functools.partial(_wkv_kernel, npairs, nchunks),
        grid=(batch, c // width, nb),
        in_specs=[fwd_spec] * 6 + [bwd_spec] * 6,
        out_specs=[fwd_spec, bwd_spec],
        out_shape=[jax.ShapeDtypeStruct((n, c), F32)] * 2,
        scratch_shapes=[pltpu.VMEM((npairs, PAIR, PAIR), F32)] * 2,
        compiler_params=_params(("parallel", "parallel", "arbitrary")),
        name="wkv_scan",
    )(*ins)


def _mlp_tail(x1, gm_ref, wup_ref, wdn_ref, out_ref):
    hb = _rms(x1, gm_ref[...]).astype(BF16)
    c = x1.shape[1]
    acc = x1
    nblk = wup_ref.shape[1] // c
    u = _dot(hb, wup_ref[:, 0:c])
    for j in range(nblk):
        u_next = _dot(hb, wup_ref[:, (j + 1) * c:(j + 2) * c]) if j + 1 < nblk else None
        u = jnp.maximum(u, 0.0)
        acc = acc + _dot((u * u).astype(BF16), wdn_ref[j * c:(j + 1) * c, :])
        u = u_next
    out_ref[...] = acc


def _na_post_kernel(x_ref, o_ref, wo_ref, gm_ref, wup_ref, wdn_ref, out_ref):
    x1 = x_ref[...] + _dot(o_ref[...], wo_ref[...])
    _mlp_tail(x1, gm_ref, wup_ref, wdn_ref, out_ref)


def _rwkv_post_kernel(x_ref, yf_ref, yb_ref, bv_ref, g_ref, ln_ref, e_ref, wo_ref, gm_ref, wup_ref,
                      wdn_ref, out_ref):
    e = e_ref[...]
    y = yf_ref[...] + yb_ref[...]
    inv_n = 1.0 / HEAD_DIM
    d = y - _segsum(y, e) * inv_n
    var = _segsum(d * d, e) * inv_n
    yn = d * lax.rsqrt(var + GN_EPS) * ln_ref[0:1, :] + ln_ref[1:2, :] + bv_ref[...]
    x1 = x_ref[...] + _dot((yn * g_ref[...]).astype(BF16), wo_ref[...])
    _mlp_tail(x1, gm_ref, wup_ref, wdn_ref, out_ref)


def _post(kernel, x, acts, consts, tm):
    n, c = x.shape
    row_spec = pl.BlockSpec((tm, c), lambda i: (i, 0))
    ins = [x] + list(acts) + list(consts)
    specs = [row_spec] * (1 + len(acts)) + [_const_spec(a.shape) for a in consts]
    return pl.pallas_call(
        kernel,
        grid=(n // tm,),
        in_specs=specs,
        out_specs=row_spec,
        out_shape=jax.ShapeDtypeStruct((n, c), F32),
        compiler_params=_params(("parallel",)),
        name=kernel.__name__.strip("_"),
    )(*ins)


def _na_qkv_kernel(x_ref, gn_ref, w_ref, qg_ref, kg_ref, e_ref, q_o, k_o, v_o):
    hb = _rms(x_ref[...], gn_ref[...]).astype(BF16)
    c = x_ref.shape[1]
    e = e_ref[...]
    inv_n = 1.0 / HEAD_DIM
    blocks = [(out, gain, part * c + j * SEG, j * SEG)
              for part, (out, gain) in enumerate(((q_o, qg_ref), (k_o, kg_ref), (v_o, None)))
              for j in range(c // SEG)]

    def finish(acc, out, gain, col):
        if gain is not None:
            ms = _segsum(acc * acc, e) * inv_n
            acc = acc * lax.rsqrt(ms + RMS_EPS) * gain[...]
        out[:, col:col + SEG] = acc.astype(out.dtype)

    pending = None
    for out, gain, wcol, col in blocks:
        acc = _dot(hb, w_ref[:, wcol:wcol + SEG])
        if pending is not None:
            finish(*pending)
        pending = (acc, out, gain, col)
    finish(*pending)


def _na_qkv(x, p, tm):
    n, c = x.shape
    row_spec = pl.BlockSpec((tm, c), lambda i: (i, 0))
    consts = [p['gn'], p['wqkv'], p['qg'], p['kg'], p['e']]
    return pl.pallas_call(
        _na_qkv_kernel,
        grid=(n // tm,),
        in_specs=[row_spec] + [_const_spec(a.shape) for a in consts],
        out_specs=[row_spec] * 3,
        out_shape=[jax.ShapeDtypeStruct((n, c), BF16)] * 3,
        compiler_params=_params(("parallel",)),
        name="na_qkv",
    )(x, *consts)


def _na_kernel(rows, q_ref, kp_ref, kc_ref, kn_ref, vp_ref, vc_ref, vn_ref, bias_ref, o_ref, kw_ref, vw_ref):
    ib = pl.program_id(2)
    blk = NA_ROWS * GRID_W
    for j, (kr, vr) in enumerate(((kp_ref, vp_ref), (kc_ref, vc_ref), (kn_ref, vn_ref))):
        kw_ref[j * blk:(j + 1) * blk, :] = kr[...]
        vw_ref[j * blk:(j + 1) * blk, :] = vr[...]
    lane = lax.broadcasted_iota(jnp.int32, (GRID_W, PAIR), 1)
    head0 = lane < HEAD_DIM
    nkeys = WIN_ROWS * GRID_W
    probs = []
    for p in range(NA_PAIRS):
        lanes = slice(p * PAIR, (p + 1) * PAIR)
        for ii in range(NA_ROWS):
            i = ib * NA_ROWS + ii
            r0 = jnp.clip(i - WIN_ROWS // 2, 0, rows - WIN_ROWS)
            start = pl.multiple_of((r0 - ib * NA_ROWS + NA_ROWS) * GRID_W, GRID_W)
            probs.append((p, ii, lanes, pl.ds(start, nkeys), r0 - i + WIN_ROWS - 1))
    qss = []
    for p, ii, lanes, _, _ in probs:
        q = q_ref[ii * GRID_W:(ii + 1) * GRID_W, lanes]
        zero = jnp.zeros_like(q)
        qss.append(jnp.concatenate([jnp.where(head0, q, zero), jnp.where(head0, zero, q)], axis=0))
    ss = [_dot_nt(qs, kw_ref[win, lanes])
          + jnp.concatenate([bias_ref[p, d0 + 2 * j] for j in range(WIN_ROWS // 2)], axis=1)
          for qs, (p, _, lanes, win, d0) in zip(qss, probs)]
    prs = [jnp.exp2(s - jnp.max(s, axis=-1, keepdims=True)) for s in ss]
    inv_ls = [1.0 / jnp.sum(pr, axis=-1, keepdims=True) for pr in prs]
    pvs = [_dot(pr.astype(BF16), vw_ref[win, lanes]) * inv_l
           for pr, inv_l, (_, _, lanes, win, _) in zip(prs, inv_ls, probs)]
    for pv, (_, ii, lanes, _, _) in zip(pvs, probs):
        o = jnp.where(head0, pv[:GRID_W], pv[GRID_W:])
        o_ref[ii * GRID_W:(ii + 1) * GRID_W, lanes] = o.astype(o_ref.dtype)


def _na(q, k, v, bias, batch, seq_len):
    n, c = q.shape
    rows = seq_len // GRID_W
    nrb = rows // NA_ROWS
    blk = NA_ROWS * GRID_W
    width = NA_PAIRS * PAIR

    def spec(off):
        return pl.BlockSpec((blk, width), lambda p, bi, ib: (bi * nrb + jnp.clip(ib + off, 0, nrb - 1), p))

    bias_spec = pl.BlockSpec((NA_PAIRS,) + bias.shape[1:], lambda p, bi, ib: (p, 0, 0, 0))
    return pl.pallas_call(
        functools.partial(_na_kernel, rows),
        grid=(c // width, batch, nrb),
        in_specs=[spec(0), spec(-1), spec(0), spec(1), spec(-1), spec(0), spec(1), bias_spec],
        out_specs=spec(0),
        out_shape=jax.ShapeDtypeStruct((n, c), BF16),
        scratch_shapes=[pltpu.VMEM((3 * blk, width), BF16)] * 2,
        compiler_params=_params(("parallel", "parallel", "arbitrary")),
        name="na_attn",
    )(q, k, k, k, v, v, v, bias)


def _na_bias(rpb):
    nh, ndr = rpb.shape[:2]
    qc = np.arange(GRID_W)[:, None]
    kc = np.arange(GRID_W)[None, :]
    win_start = np.clip(qc - WIN_COLS // 2, 0, GRID_W - WIN_COLS)
    valid = (kc >= win_start) & (kc < win_start + WIN_COLS)
    dc = np.clip(kc - qc + WIN_COLS - 1, 0, 2 * WIN_COLS - 2)
    tab = jnp.where(valid[None, None], rpb[:, :, dc] * LOG2E, NEG_BIG)
    two = jnp.concatenate([tab[:, :-1], tab[:, 1:]], axis=-1)
    two = two.reshape(nh // 2, 2, ndr - 1, GRID_W, PAIR)
    return jnp.transpose(two, (0, 2, 1, 3, 4)).reshape(nh // 2, ndr - 1, PAIR, PAIR)


def _block_diag2(w):
    k, c = w.shape[1:]
    z = jnp.zeros((k, c), w.dtype)
    return jnp.concatenate([jnp.concatenate([w[0], z], 1), jnp.concatenate([z, w[1]], 1)], 0)


def _pad_to(w, axis, size):
    pad = [(0, 0)] * w.ndim
    pad[axis] = (0, size - w.shape[axis])
    return jnp.pad(w, pad)


def _seg_matrix():
    i = np.arange(SEG) // HEAD_DIM
    return jnp.asarray(i[:, None] == i[None, :], BF16)


def _pack_rwkv(j, w):
    c = w['rw_rkv'].shape[-1]
    p = {}
    p['mix'] = _pad_to(w['rw_mix'][j], 0, 8)
    p['wrkv'] = w['rw_rkv'][j].astype(BF16)
    dn = [w['rw_w1'][j, 0], w['rw_w1'][j, 1], w['rw_a1'][j, 0], w['rw_a1'][j, 1],
          _pad_to(0.5 * w['rw_g1'][j], 1, 256)]
    k_a = w['rw_ka'][j]
    vec = [0.5 * w['rw_w0'][j, 0], 0.5 * w['rw_w0'][j, 1], 0.5 * w['rw_a0'][j, 0], 0.5 * w['rw_a0'][j, 1],
           w['rw_kk'][j], 1.0 - 0.5 * k_a, w['rw_rk'][j].reshape(c), 0.5 * k_a]
    if j > 0:
        dn.append(_pad_to(w['rw_v1'][j - 1], 1, 128))
        vec.append(0.5 * w['rw_v0'][j - 1])
        p['v2'] = _pad_to(0.5 * w['rw_v2'][j - 1], 0, 128).astype(BF16)
    p['dn'] = jnp.concatenate(dn, axis=1).astype(BF16)
    p['vec'] = _pad_to(jnp.stack(vec), 0, 16)
    p['w2'] = _block_diag2(0.5 * w['rw_w2'][j]).astype(BF16)
    p['a2'] = _block_diag2(0.5 * w['rw_a2'][j]).astype(BF16)
    p['g2'] = _pad_to(w['rw_g2'][j], 0, 256).astype(BF16)
    p['ln'] = _pad_to(jnp.stack([w['rw_lnx_g'][j], w['rw_lnx_b'][j]]), 0, 8)
    p['wo'] = w['rw_o'][j].astype(BF16)
    return p


def _pack_na(j, w):
    scale = HEAD_DIM ** -0.5 * LOG2E
    return {
        'wqkv': w['na_qkv'][j].astype(BF16),
        'qg': jnp.tile(w['na_q_g'][j] * scale, SEG // HEAD_DIM)[None, :],
        'kg': jnp.tile(w['na_k_g'][j], SEG // HEAD_DIM)[None, :],
        'bias': _na_bias(w['na_rpb'][j]),
        'wo': w['na_o'][j].astype(BF16),
    }


def _trunk(x3, w, depth):
    batch, seq_len, c = x3.shape
    x = x3.reshape(batch * seq_len, c)
    e = _seg_matrix()
    v_first = None
    for layer in range(depth):
        j = layer // 2
        gn = w['norm_mix_g'][layer][None, :]
        mlp = [w['norm_mlp_g'][layer][None, :], w['w_up'][layer].astype(BF16), w['w_down'][layer].astype(BF16)]
        if layer % 2 == 0:
            p = _pack_rwkv(j, w)
            p['gn'], p['e'] = gn, e
            r, v, kk, bv, g, lw0, lw1, kd0, kd1, b0, b1 = _rwkv_pre(x, v_first, p, seq_len, 256)
            if v_first is None:
                v_first = v
            y_f, y_b = _wkv(r, v, kk, lw0, lw1, kd0, kd1, b0, b1, batch, seq_len, WKV_PAIRS, WKV_CHUNKS)
            x = _post(_rwkv_post_kernel, x, [y_f, y_b, bv, g], [p['ln'], e, p['wo']] + mlp, 256)
        else:
            p = _pack_na(j, w)
            p['gn'], p['e'] = gn, e
            q, k, v = _na_qkv(x, p, 512)
            o = _na(q, k, v, p['bias'], batch, seq_len)
            x = _post(_na_post_kernel, x, [o], [p['wo']] + mlp, 512)
    return x.reshape(batch, seq_len, c)


def kernel(x_prompt, x_sample, norm_mix_g, norm_mlp_g, w_up, w_down, rw_mix, rw_rkv, rw_w0, rw_w1, rw_w2,
           rw_a0, rw_a1, rw_a2, rw_v0, rw_v1, rw_v2, rw_g1, rw_g2, rw_kk, rw_ka, rw_rk, rw_lnx_g, rw_lnx_b,
           rw_o, na_qkv, na_q_g, na_k_g, na_rpb, na_o):
    w = dict(norm_mix_g=norm_mix_g, norm_mlp_g=norm_mlp_g, w_up=w_up, w_down=w_down,
             rw_mix=rw_mix, rw_rkv=rw_rkv, rw_w0=rw_w0, rw_w1=rw_w1, rw_w2=rw_w2,
             rw_a0=rw_a0, rw_a1=rw_a1, rw_a2=rw_a2, rw_v0=rw_v0, rw_v1=rw_v1, rw_v2=rw_v2,
             rw_g1=rw_g1, rw_g2=rw_g2, rw_kk=rw_kk, rw_ka=rw_ka, rw_rk=rw_rk,
             rw_lnx_g=rw_lnx_g, rw_lnx_b=rw_lnx_b, rw_o=rw_o,
             na_qkv=na_qkv, na_q_g=na_q_g, na_k_g=na_k_g, na_rpb=na_rpb, na_o=na_o)
    depth = norm_mix_g.shape[0]
    return _trunk(x_prompt, w, depth), _trunk(x_sample, w, depth)
```
